```python
import math
import jax, jax.numpy as jnp
from jax import lax
import numpy as np

D_MODEL = 1024
BATCH = 16
SEQ = 2048
DEPTH = 4

GRID_W = 64
CTX_LEN = 256
N_MIXERS = 3
N_A = (DEPTH + 2) // 3
N_B = (DEPTH + 1) // 3
N_C = DEPTH // 3
ALPHA = (2.0 * DEPTH) ** 0.25
BETA_INIT = (8.0 * DEPTH) ** -0.25
D_FF = 4 * D_MODEL
EPS = 1e-6
LN_EPS = 1e-5

H_A = 8
DK_A = D_MODEL // H_A
DV_A = D_MODEL // H_A
KEY_A = H_A * DK_A
VAL_A = H_A * DV_A
CONV_CH_A = 2 * KEY_A + VAL_A
IN_A = CONV_CH_A + VAL_A + 4 * H_A
CONV_K = 5
CHUNK_A = 64

H_B = 8
DH_B = D_MODEL // (2 * H_B)
DV_B = 2 * DH_B
QK_B = 2 * H_B * DH_B
IN_B = 2 * QK_B + H_B * DV_B
Q_BLOCK = 128
ROPE_BASE = 10000.0

H_C = 4
DQK_C = D_MODEL // (2 * H_C)
DV_C = D_MODEL // H_C
QK_C = H_C * DQK_C
VAL_C = H_C * DV_C
IN_C = 2 * QK_C + 2 * VAL_C + 4 * H_C
CHUNK_C = 64

kernel_name = 'hybrid_gdn_diffattn_mlstm_deepnorm_dit'


def layer_norm(x, g, b):
    xf = x.astype(jnp.float32)
    mu = jnp.mean(xf, -1, keepdims=True)
    var = jnp.mean(jnp.square(xf - mu), -1, keepdims=True)
    return ((xf - mu) * lax.rsqrt(var + LN_EPS) * g.astype(jnp.float32) + b.astype(jnp.float32)).astype(x.dtype)


def rms_norm(x, g):
    xf = x.astype(jnp.float32)
    return (xf * lax.rsqrt(jnp.mean(xf * xf, -1, keepdims=True) + EPS) * g.astype(jnp.float32)).astype(x.dtype)


def l2_normalize(x):
    xf = x.astype(jnp.float32)
    return xf * lax.rsqrt(jnp.sum(xf * xf, -1, keepdims=True) + EPS)


def modulate(x, shift, scale):
    return x * (1.0 + scale) + shift


def centred_dwconv(x, w):
    k = w.shape[0]
    return lax.conv_general_dilated(x, w[:, None, :].astype(x.dtype), window_strides=(1,),
                                    padding=((k // 2, k // 2),), dimension_numbers=('NWC', 'WIO', 'NWC'),
                                    feature_group_count=x.shape[-1])


def sq_relu_mlp(h, w1, w2):
    return jnp.square(jax.nn.relu(h @ w1)) @ w2


def gated_delta_chunked(q, k, v, beta, g, s0):
    b, h, n, _ = q.shape
    L = CHUNK_A
    nc = n // L
    r = lambda t: t.reshape((b, h, nc, L) + t.shape[3:])
    q, k, v, beta, g = r(q), r(k), r(v), r(beta), r(g)
    g = jnp.cumsum(g, -1)
    tri = jnp.tril(jnp.ones((L, L), bool))
    strict = jnp.tril(jnp.ones((L, L), bool), -1)
    diff = g[..., :, None] - g[..., None, :]
    decay = jnp.where(tri, jnp.exp(jnp.where(tri, diff, 0.0)), 0.0)
    kb = k * beta[..., None]
    a_mat = jnp.where(strict, jnp.einsum('bhcid,bhcjd->bhcij', kb, k) * decay, 0.0)
    u = lax.linalg.triangular_solve(a_mat, v * beta[..., None], left_side=True, lower=True, unit_diagonal=True)
    w = lax.linalg.triangular_solve(a_mat, kb * jnp.exp(g)[..., None], left_side=True, lower=True, unit_diagonal=True)
    qk = jnp.where(tri, jnp.einsum('bhcid,bhcjd->bhcij', q, k) * decay, 0.0)
    qg = q * jnp.exp(g)[..., None]
    kdec = k * jnp.exp(g[..., -1:] - g)[..., None]
    g_last = jnp.exp(g[..., -1])

    def step(s, xs):
        u_c, w_c, qk_c, qg_c, kdec_c, gl_c = xs
        v_new = u_c - jnp.einsum('bhld,bhdv->bhlv', w_c, s)
        o = jnp.einsum('bhld,bhdv->bhlv', qg_c, s) + jnp.einsum('bhlm,bhmv->bhlv', qk_c, v_new)
        s = s * gl_c[..., None, None] + jnp.einsum('bhld,bhlv->bhdv', kdec_c, v_new)
        return s, o

    xs = tuple(jnp.moveaxis(t, 2, 0) for t in (u, w, qk, qg, kdec, g_last))
    s_fin, o = lax.scan(step, s0, xs)
    o = jnp.moveaxis(o, 0, 2).reshape(b, h, n, v.shape[-1])
    return o, s_fin


def gated_deltanet(hl, hc, w_in, conv_w, a_log, dt_bias, norm_g, w_out, need_ctx):
    f32 = jnp.float32

    def prepare(hh):
        bsz, n, _ = hh.shape
        p = hh @ w_in
        qkv = jax.nn.silu(centred_dwconv(p[..., :CONV_CH_A], conv_w))
        q = l2_normalize(qkv[..., :KEY_A].reshape(bsz, n, H_A, DK_A)) * (DK_A ** -0.5)
        k = l2_normalize(qkv[..., KEY_A:2 * KEY_A].reshape(bsz, n, H_A, DK_A))
        v = qkv[..., 2 * KEY_A:].reshape(bsz, n, H_A, DV_A).astype(f32)
        z = p[..., CONV_CH_A:CONV_CH_A + VAL_A].reshape(bsz, n, H_A, DV_A)
        gp = p[..., CONV_CH_A + VAL_A:].reshape(bsz, n, 2, 2, H_A).astype(f32)
        g = -jnp.exp(a_log.astype(f32)) * jax.nn.softplus(gp[:, :, :, 0] + dt_bias.astype(f32))
        beta = jax.nn.sigmoid(gp[:, :, :, 1])
        tr = lambda t: jnp.swapaxes(t, 1, 2)
        return tr(q), tr(k), tr(v), jnp.moveaxis(beta, 1, -1), jnp.moveaxis(g, 1, -1), z

    def bidir(q, k, v, beta, g, s_f, s_b):
        flip = lambda t: jnp.flip(t, axis=2)
        o_f, s_f = gated_delta_chunked(q, k, v, beta[:, 0], g[:, 0], s_f)
        o_b, s_b = gated_delta_chunked(flip(q), flip(k), flip(v), flip(beta[:, 1]), flip(g[:, 1]), s_b)
        return o_f + flip(o_b), s_f, s_b

    def finish(o, z):
        o = jnp.swapaxes(o, 1, 2).astype(z.dtype)
        o = rms_norm(o, norm_g) * jax.nn.silu(z)
        return o.reshape(o.shape[0], o.shape[1], VAL_A) @ w_out

    qc, kc, vc, bc, gc, zc = prepare(hc)
    s0 = jnp.zeros((hc.shape[0], H_A, DK_A, DV_A), f32)
    oc, s_f, s_b = bidir(qc, kc, vc, bc, gc, s0, s0)
    ql, kl, vl, bl, gl, zl = prepare(hl)
    ol, _, _ = bidir(ql, kl, vl, bl, gl, s_f, s_b)
    yl = finish(ol, zl)
    yc = finish(oc, zc) if need_ctx else None
    return yl, yc


def axial_rope_tables(n):
    rows = n // GRID_W
    r = jnp.repeat(jnp.arange(rows, dtype=jnp.float32), GRID_W)
    col = jnp.tile(jnp.arange(GRID_W, dtype=jnp.float32), rows)
    half = DH_B // 2
    inv = ROPE_BASE ** (-jnp.arange(0, half, 2, dtype=jnp.float32) / half)
    ang = jnp.concatenate([r[:, None] * inv, col[:, None] * inv], -1)
    return jnp.cos(ang), jnp.sin(ang)


def apply_axial_rope(x, cos, sin):
    fq = DH_B // 4
    xa = x.reshape(x.shape[:-1] + (2, 2, fq))
    x1, x2 = xa[..., 0, :], xa[..., 1, :]
    c = cos.reshape(cos.shape[0], 1, 1, 2, fq).astype(x.dtype)
    s = sin.reshape(sin.shape[0], 1, 1, 2, fq).astype(x.dtype)
    out = jnp.stack([x1 * c - x2 * s, x2 * c + x1 * s], -2)
    return out.reshape(x.shape)


def diff_attention(hl, hc, w_in, lam, subln, w_out, layer_idx, need_ctx):
    b, n, _ = hl.shape
    lam_init = 0.8 - 0.6 * math.exp(-0.3 * layer_idx)
    lf = lam.astype(jnp.float32)
    lam_full = jnp.exp(jnp.sum(lf[0] * lf[1])) - jnp.exp(jnp.sum(lf[2] * lf[3])) + lam_init
    scale = DH_B ** -0.5

    def project(hh):
        bsz, m, _ = hh.shape
        p = hh @ w_in
        q = p[..., :QK_B].reshape(bsz, m, H_B, 2, DH_B)
        k = p[..., QK_B:2 * QK_B].reshape(bsz, m, H_B, 2, DH_B)
        v = p[..., 2 * QK_B:].reshape(bsz, m, H_B, DV_B)
        return q, k, v

    def attend(q, k, v):
        s = jnp.einsum('bqhtd,bkhtd->bhtqk', q, k).astype(jnp.float32) * scale
        pr = jax.nn.softmax(s, axis=-1)
        a = pr[:, :, 0] - lam_full * pr[:, :, 1]
        return jnp.einsum('bhqk,bkhd->bqhd', a.astype(v.dtype), v)

    def finish(o):
        o = rms_norm(o, subln) * (1.0 - lam_init)
        return o.reshape(o.shape[0], o.shape[1], H_B * DV_B) @ w_out

    ql, kl, vl = project(hl)
    qc, kc, vc = project(hc)
    cos, sin = axial_rope_tables(n)
    ql = apply_axial_rope(ql, cos, sin)
    kl = apply_axial_rope(kl, cos, sin)
    k_all = jnp.concatenate([kc, kl], axis=1)
    v_all = jnp.concatenate([vc, vl], axis=1)
    qb = jnp.moveaxis(ql.reshape(b, n // Q_BLOCK, Q_BLOCK, H_B, 2, DH_B), 1, 0)
    ol = lax.map(lambda qblk: attend(qblk, k_all, v_all), qb)
    ol = jnp.moveaxis(ol, 0, 1).reshape(b, n, H_B, DV_B)
    yl = finish(ol)
    yc = finish(attend(qc, kc, vc)) if need_ctx else None
    return yl, yc


def mlstm_chunked(q, k, v, log_i, log_f, state):
    b, h, n, _ = q.shape
    L = CHUNK_C
    nc = n // L
    r = lambda t: t.reshape((b, h, nc, L) + t.shape[3:])
    q, k, v, log_i, log_f = r(q), r(k), r(v), r(log_i), r(log_f)
    bcum = jnp.cumsum(log_f, -1)
    tri = jnp.tril(jnp.ones((L, L), bool))
    dmat = jnp.where(tri, bcum[..., :, None] - bcum[..., None, :] + log_i[..., None, :], -jnp.inf)
    m_intra = jnp.max(dmat, -1)
    wts = jnp.exp(dmat - m_intra[..., None]) * jnp.einsum('bhcid,bhcjd->bhcij', q, k)
    num_intra = jnp.einsum('bhcij,bhcjv->bhciv', wts, v)
    den_intra = jnp.sum(wts, -1)
    a_end = bcum[..., -1:] - bcum + log_i
    m_end = m_intra[..., -1]
    k_end = k * jnp.exp(a_end - m_end[..., None])[..., None]
    b_last = bcum[..., -1]

    def step(carry, xs):
        c_st, n_st, m0 = carry
        q_c, bcum_c, mi_c, num_c, den_c, ke_c, v_c, me_c, bl_c = xs
        m_t = jnp.maximum(bcum_c + m0[..., None], mi_c)
        s_inter = jnp.exp(bcum_c + m0[..., None] - m_t)
        s_intra = jnp.exp(mi_c - m_t)
        num = s_inter[..., None] * jnp.einsum('bhld,bhdv->bhlv', q_c, c_st) + s_intra[..., None] * num_c
        den = s_inter * jnp.einsum('bhld,bhd->bhl', q_c, n_st) + s_intra * den_c
        h_t = num / jnp.maximum(jnp.abs(den), jnp.exp(-m_t))[..., None]
        m_new = jnp.maximum(bl_c + m0, me_c)
        s_old = jnp.exp(bl_c + m0 - m_new)
        s_new = jnp.exp(me_c - m_new)
        c_st = s_old[..., None, None] * c_st + s_new[..., None, None] * jnp.einsum('bhld,bhlv->bhdv', ke_c, v_c)
        n_st = s_old[..., None] * n_st + s_new[..., None] * jnp.sum(ke_c, -2)
        return (c_st, n_st, m_new), h_t

    xs = tuple(jnp.moveaxis(t, 2, 0) for t in (q, bcum, m_intra, num_intra, den_intra, k_end, v, m_end, b_last))
    state, hs = lax.scan(step, state, xs)
    hs = jnp.moveaxis(hs, 0, 2).reshape(b, h, n, v.shape[-1])
    return hs, state


def mlstm_mixer(hl, hc, w_in, gate_bias, norm_g, w_out, need_ctx):
    f32 = jnp.float32

    def prepare(hh):
        bsz, n, _ = hh.shape
        p = hh @ w_in
        tr = lambda t: jnp.swapaxes(t, 1, 2).astype(f32)
        q = tr(p[..., :QK_C].reshape(bsz, n, H_C, DQK_C))
        k = tr(p[..., QK_C:2 * QK_C].reshape(bsz, n, H_C, DQK_C)) * (DQK_C ** -0.5)
        v = tr(p[..., 2 * QK_C:2 * QK_C + VAL_C].reshape(bsz, n, H_C, DV_C))
        o = p[..., 2 * QK_C + VAL_C:2 * QK_C + 2 * VAL_C].reshape(bsz, n, H_C, DV_C)
        gp = (p[..., 2 * QK_C + 2 * VAL_C:].reshape(bsz, n, 4, H_C) + gate_bias).astype(f32)
        gp = jnp.moveaxis(gp, 1, -1)
        log_i = gp[:, 0::2]
        log_f = jax.nn.log_sigmoid(gp[:, 1::2])
        return q, k, v, log_i, log_f, o

    def bidir(q, k, v, log_i, log_f, st_f, st_b):
        flip = lambda t: jnp.flip(t, axis=2)
        h_f, st_f = mlstm_chunked(q, k, v, log_i[:, 0], log_f[:, 0], st_f)
        h_b, st_b = mlstm_chunked(flip(q), flip(k), flip(v), flip(log_i[:, 1]), flip(log_f[:, 1]), st_b)
        return h_f + flip(h_b), st_f, st_b

    def finish(hh, o):
        hh = jnp.swapaxes(hh, 1, 2).astype(o.dtype)
        hh = rms_norm(hh, norm_g.reshape(H_C, DV_C)) * jax.nn.sigmoid(o)
        return hh.reshape(hh.shape[0], hh.shape[1], VAL_C) @ w_out

    qc, kc, vc, ic, fc, oc_gate = prepare(hc)
    bsz = hc.shape[0]
    st0 = (jnp.zeros((bsz, H_C, DQK_C, DV_C), f32), jnp.zeros((bsz, H_C, DQK_C), f32), jnp.zeros((bsz, H_C), f32))
    hc_out, st_f, st_b = bidir(qc, kc, vc, ic, fc, st0, st0)
    ql, kl, vl, il, fl, ol_gate = prepare(hl)
    hl_out, _, _ = bidir(ql, kl, vl, il, fl, st_f, st_b)
    yl = finish(hl_out, ol_gate)
    yc = finish(hc_out, oc_gate) if need_ctx else None
    return yl, yc


def setup_inputs(seed: int = 0) -> dict:
    key = jax.random.key(seed)
    ks = jax.random.split(key, 24)
    f32 = jnp.float32

    def nrm(k, shape, scale):
        return jax.random.normal(k, shape, f32) * scale

    d = D_MODEL
    x = nrm(ks[0], (BATCH, SEQ, d), 1.0)
    c = nrm(ks[1], (BATCH, d), 1.0)
    ctx = nrm(ks[2], (BATCH, CTX_LEN, d), 1.0)
    c_ctx = nrm(ks[3], (d,), 1.0)
    w_mod = nrm(ks[4], (DEPTH, d, 6 * d), d ** -0.5)
    b_mod = nrm(ks[5], (DEPTH, 6 * d), 0.02)
    ln_g = 1.0 + nrm(ks[6], (DEPTH, 2, d), 0.02)
    ln_b = nrm(ks[7], (DEPTH, 2, d), 0.02)
    w_in_a = nrm(ks[8], (N_A, d, IN_A), d ** -0.5)
    conv_a = nrm(ks[9], (N_A, CONV_K, CONV_CH_A), CONV_K ** -0.5)
    a_log_a = jnp.log(jax.random.uniform(ks[10], (N_A, 2, H_A), f32, minval=1.0, maxval=16.0))
    dt = jnp.exp(jax.random.uniform(ks[11], (N_A, 2, H_A), f32, minval=math.log(1e-3), maxval=math.log(1e-1)))
    dt_bias_a = dt + jnp.log(-jnp.expm1(-dt))
    norm_a = 1.0 + nrm(ks[12], (N_A, DV_A), 0.02)
    w_out_a = nrm(ks[13], (N_A, VAL_A, d), BETA_INIT * VAL_A ** -0.5)
    w_in_b = nrm(ks[14], (N_B, d, IN_B), d ** -0.5)
    lam_b = nrm(ks[15], (N_B, 4, DH_B), 0.1)
    subln_b = 1.0 + nrm(ks[16], (N_B, DV_B), 0.02)
    w_out_b = nrm(ks[17], (N_B, H_B * DV_B, d), BETA_INIT * (H_B * DV_B) ** -0.5)
    w_in_c = nrm(ks[18], (N_C, d, IN_C), d ** -0.5)
    f_bias = jnp.linspace(3.0, 6.0, H_C, dtype=f32)
    base = jnp.stack([jnp.zeros_like(f_bias), f_bias, jnp.zeros_like(f_bias), f_bias])
    gate_bias_c = base + nrm(ks[19], (N_C, 4, H_C), 0.1)
    norm_c = 1.0 + nrm(ks[20], (N_C, VAL_C), 0.02)
    w_out_c = nrm(ks[21], (N_C, VAL_C, d), BETA_INIT * VAL_C ** -0.5)
    w1 = nrm(ks[22], (DEPTH, d, D_FF), d ** -0.5)
    w2 = nrm(ks[23], (DEPTH, D_FF, d), BETA_INIT * D_FF ** -0.5)
    return {'x': x, 'c': c, 'ctx': ctx, 'c_ctx': c_ctx, 'w_mod': w_mod, 'b_mod': b_mod,
            'ln_g': ln_g, 'ln_b': ln_b, 'w_in_a': w_in_a, 'conv_a': conv_a, 'a_log_a': a_log_a,
            'dt_bias_a': dt_bias_a, 'norm_a': norm_a, 'w_out_a': w_out_a, 'w_in_b': w_in_b,
            'lam_b': lam_b, 'subln_b': subln_b, 'w_out_b': w_out_b, 'w_in_c': w_in_c,
            'gate_bias_c': gate_bias_c, 'norm_c': norm_c, 'w_out_c': w_out_c, 'w1': w1, 'w2': w2}


def reference(x, c, ctx, c_ctx, w_mod, b_mod, ln_g, ln_b, w_in_a, conv_a, a_log_a, dt_bias_a, norm_a, w_out_a,
              w_in_b, lam_b, subln_b, w_out_b, w_in_c, gate_bias_c, norm_c, w_out_c, w1, w2):
    xl, xc = x, ctx
    s_lat = jax.nn.silu(c)
    s_ctx = jax.nn.silu(c_ctx)
    for i in range(DEPTH):
        need_ctx = i < DEPTH - 1
        kind, j = i % N_MIXERS, i // N_MIXERS
        m_lat = jnp.split((s_lat @ w_mod[i] + b_mod[i])[:, None, :], 6, axis=-1)
        m_ctx = jnp.split(s_ctx @ w_mod[i] + b_mod[i], 6, axis=-1)
        hl = modulate(xl, m_lat[0], m_lat[1])
        hc = modulate(xc, m_ctx[0], m_ctx[1])
        if kind == 0:
            yl, yc = gated_deltanet(hl, hc, w_in_a[j], conv_a[j], a_log_a[j], dt_bias_a[j], norm_a[j], w_out_a[j], need_ctx)
        elif kind == 1:
            yl, yc = diff_attention(hl, hc, w_in_b[j], lam_b[j], subln_b[j], w_out_b[j], i, need_ctx)
        else:
            yl, yc = mlstm_mixer(hl, hc, w_in_c[j], gate_bias_c[j], norm_c[j], w_out_c[j], need_ctx)
        xl = layer_norm(ALPHA * xl + m_lat[2] * yl, ln_g[i, 0], ln_b[i, 0])
        xl = layer_norm(ALPHA * xl + m_lat[5] * sq_relu_mlp(modulate(xl, m_lat[3], m_lat[4]), w1[i], w2[i]),
                        ln_g[i, 1], ln_b[i, 1])
        if need_ctx:
            xc = layer_norm(ALPHA * xc + m_ctx[2] * yc, ln_g[i, 0], ln_b[i, 0])
            xc = layer_norm(ALPHA * xc + m_ctx[5] * sq_relu_mlp(modulate(xc, m_ctx[3], m_ctx[4]), w1[i], w2[i]),
                            ln_g[i, 1], ln_b[i, 1])
    return xl
```

```python
import functools
import math

import jax
import jax.numpy as jnp
from jax import lax
from jax.experimental import pallas as pl
from jax.experimental.pallas import tpu as pltpu

F32 = jnp.float32
BF16 = jnp.bfloat16
HIGHEST = lax.Precision.HIGHEST

DEPTH = 4
N_MIXERS = 3
ALPHA = (2.0 * DEPTH) ** 0.25
EPS = 1e-6
LN_EPS = 1e-5
GRID_W = 64
ROPE_BASE = 10000.0
CHUNK = 64
LANES = 128

H_A, DK_A = 8, 128
CONV_K = 5
H_B, DH_B = 8, 64
H_C, DQK_C, DV_C = 4, 128, 256

VMEM_LIMIT = 56 * 1024 * 1024


def _params(sem, vmem=VMEM_LIMIT):
    return pltpu.CompilerParams(dimension_semantics=sem, vmem_limit_bytes=vmem)


def _sigmoid(x):
    return 1.0 / (1.0 + jnp.exp(-x))


def _softplus(x):
    return jnp.maximum(x, 0.0) + jnp.log1p(jnp.exp(-jnp.abs(x)))


def _dot(a, b):
    return jnp.dot(a, b, preferred_element_type=F32)


def _dot_nt(a, b):
    return lax.dot_general(a, b, (((1,), (1,)), ((), ())), preferred_element_type=F32)


def _dot_tn(a, b):
    return lax.dot_general(a, b, (((0,), (0,)), ((), ())), preferred_element_type=F32)


def _row_tile(nt):
    for tm in (768, 512, 256):
        if nt % tm == 0:
            return tm
    raise ValueError(f"sequence length {nt} must be a multiple of 256")


def _mod_kernel(c_ref, w_ref, b_ref, o_ref):
    cc = c_ref[...]
    s = cc * _sigmoid(cc)
    o_ref[...] = jnp.dot(s, w_ref[...], precision=HIGHEST, preferred_element_type=F32) + b_ref[...]


def _mod_tables(cc, w_mod, b_mod):
    r, d = cc.shape
    depth, _, f = w_mod.shape
    tn = 1536
    out = pl.pallas_call(
        _mod_kernel,
        grid=(depth, f // tn),
        in_specs=[pl.BlockSpec((r, d), lambda l, n: (0, 0)),
                  pl.BlockSpec((None, d, tn), lambda l, n: (l, 0, n)),
                  pl.BlockSpec((None, 1, tn), lambda l, n: (l, 0, n))],
        out_specs=pl.BlockSpec((None, r, tn), lambda l, n: (l, 0, n)),
        out_shape=jax.ShapeDtypeStruct((depth, r, f), F32),
        compiler_params=_params(("parallel", "parallel")),
        name="mod_tables",
    )(cc, w_mod, b_mod.reshape(depth, 1, f))
    return out.reshape(depth, r, 6, d)


def _mod_specs(layer, ctx_row):
    def lat(b, *_):
        return (layer, b, 0, 0)

    def ctx(b, *_):
        return (layer, ctx_row, 0, 0)
    return lat, ctx


def _select_mod(ml_ref, mc_ref, idx, is_ctx):
    return jnp.where(is_ctx, mc_ref[idx:idx + 1, :], ml_ref[idx:idx + 1, :])


def _inproj_kernel(x_ref, ml_ref, mc_ref, w_ref, *rest, tm, n_lat, rope_blocks):
    if rope_blocks:
        cos_ref, sin_ref, o_ref, h_ref = rest
    else:
        o_ref, h_ref = rest
    j = pl.program_id(1)
    n = pl.program_id(2)

    @pl.when(n == 0)
    def _():
        row = j * tm + lax.broadcasted_iota(jnp.int32, (tm, 1), 0)
        is_ctx = row >= n_lat
        shift = _select_mod(ml_ref, mc_ref, 0, is_ctx)
        scale = _select_mod(ml_ref, mc_ref, 1, is_ctx)
        h_ref[...] = (x_ref[...] * (1.0 + scale) + shift).astype(h_ref.dtype)

    acc = _dot(h_ref[...], w_ref[...])
    if not rope_blocks:
        o_ref[...] = acc
        return

    @pl.when(n >= rope_blocks)
    def _():
        o_ref[...] = acc

    @pl.when(n < rope_blocks)
    def _():
        cos = cos_ref[...]
        sin = sin_ref[...]
        lane = lax.broadcasted_iota(jnp.int32, (1, LANES), 1)
        first_half = (lane & 31) < 16
        for i in range(acc.shape[1] // LANES):
            xb = acc[:, i * LANES:(i + 1) * LANES]
            partner = jnp.where(first_half, pltpu.roll(xb, LANES - 16, axis=1), pltpu.roll(xb, 16, axis=1))
            o_ref[:, i * LANES:(i + 1) * LANES] = xb * cos + partner * sin


def _inproj(xs, modtab, layer, w, n_lat, rope=None, rope_blocks=0):
    b, nt, d = xs.shape
    f = w.shape[1]
    tm = _row_tile(nt)
    tn = min(f, 1024)
    lat, ctx = _mod_specs(layer, b)
    in_specs = [pl.BlockSpec((None, tm, d), lambda bi, j, n: (bi, j, 0)),
                pl.BlockSpec((None, None, 6, d), lat),
                pl.BlockSpec((None, None, 6, d), ctx),
                pl.BlockSpec((d, tn), lambda bi, j, n: (0, n))]
    args = [xs, modtab, modtab, w]
    if rope_blocks:
        in_specs += [pl.BlockSpec((tm, LANES), lambda bi, j, n: (j, 0))] * 2
        args += list(rope)
    return pl.pallas_call(
        functools.partial(_inproj_kernel, tm=tm, n_lat=n_lat, rope_blocks=rope_blocks),
        grid=(b, nt // tm, f // tn),
        in_specs=in_specs,
        out_specs=pl.BlockSpec((None, tm, tn), lambda bi, j, n: (bi, j, n)),
        out_shape=jax.ShapeDtypeStruct((b, nt, f), F32),
        scratch_shapes=[pltpu.VMEM((tm, d), BF16)],
        compiler_params=_params(("parallel", "parallel", "arbitrary")),
        name="inproj",
    )(*args)


def _layer_norm(v, g, b):
    mu = jnp.mean(v, axis=-1, keepdims=True)
    vc = v - mu
    var = jnp.mean(vc * vc, axis=-1, keepdims=True)
    return vc * lax.rsqrt(var + LN_EPS) * g + b


def _post_kernel(x_ref, ml_ref, mc_ref, o_ref, *rest, tm, n_lat, group, gate, out_scale):
    if gate is None:
        gn_ref, w_ref, lg_ref, lb_ref, out_ref = rest
    else:
        z_ref, gn_ref, w_ref, lg_ref, lb_ref, out_ref = rest
    j = pl.program_id(1)
    row = j * tm + lax.broadcasted_iota(jnp.int32, (tm, 1), 0)
    is_ctx = row >= n_lat
    width = o_ref.shape[-1]
    parts = []
    for g0 in range(0, width, group):
        og = o_ref[:, g0:g0 + group]
        ms = jnp.mean(og * og, axis=-1, keepdims=True)
        yg = og * lax.rsqrt(ms + EPS) * gn_ref[:, g0:g0 + group]
        if gate == "silu":
            z = z_ref[:, g0:g0 + group]
            yg = yg * (z * _sigmoid(z))
        elif gate == "sigmoid":
            yg = yg * _sigmoid(z_ref[:, g0:g0 + group])
        else:
            yg = yg * out_scale
        parts.append(yg.astype(BF16))
    y = _dot(jnp.concatenate(parts, axis=-1), w_ref[...])
    res_gate = _select_mod(ml_ref, mc_ref, 2, is_ctx)
    out_ref[...] = _layer_norm(ALPHA * x_ref[...] + res_gate * y, lg_ref[...], lb_ref[...])


def _post(xs, modtab, layer, o, z_arr, z_block, gnorm, w_out, ln_g, ln_b, n_lat, rows, group, gate, out_scale=1.0):
    b, _, d = xs.shape
    width = o.shape[-1]
    tm = _row_tile(rows)
    lat, ctx = _mod_specs(layer, b)
    in_specs = [pl.BlockSpec((None, tm, d), lambda bi, j: (bi, j, 0)),
                pl.BlockSpec((None, None, 6, d), lat),
                pl.BlockSpec((None, None, 6, d), ctx),
                pl.BlockSpec((None, tm, width), lambda bi, j: (bi, j, 0))]
    args = [xs, modtab, modtab, o]
    if gate is not None:
        in_specs.append(pl.BlockSpec((None, tm, width), lambda bi, j: (bi, j, z_block)))
        args.append(z_arr)
    in_specs += [pl.BlockSpec((1, width), lambda bi, j: (0, 0)),
                 pl.BlockSpec((width, d), lambda bi, j: (0, 0)),
                 pl.BlockSpec((1, d), lambda bi, j: (0, 0)),
                 pl.BlockSpec((1, d), lambda bi, j: (0, 0))]
    args += [gnorm.reshape(1, width), w_out, ln_g.reshape(1, d), ln_b.reshape(1, d)]
    return pl.pallas_call(
        functools.partial(_post_kernel, tm=tm, n_lat=n_lat, group=group, gate=gate, out_scale=out_scale),
        grid=(b, rows // tm),
        in_specs=in_specs,
        out_specs=pl.BlockSpec((None, tm, d), lambda bi, j: (bi, j, 0)),
        out_shape=jax.ShapeDtypeStruct((b, rows, d), F32),
        compiler_params=_params(("parallel", "parallel")),
        name="mixer_out",
    )(*args)


def _mlp_kernel(x_ref, ml_ref, mc_ref, w1_ref, w2_ref, lg_ref, lb_ref, out_ref, h_ref, acc_ref, *, tm, n_lat):
    j = pl.program_id(1)
    k = pl.program_id(2)
    row = j * tm + lax.broadcasted_iota(jnp.int32, (tm, 1), 0)
    is_ctx = row >= n_lat

    @pl.when(k == 0)
    def _():
        shift = _select_mod(ml_ref, mc_ref, 3, is_ctx)
        scale = _select_mod(ml_ref, mc_ref, 4, is_ctx)
        h_ref[...] = (x_ref[...] * (1.0 + scale) + shift).astype(h_ref.dtype)
        acc_ref[...] = jnp.zeros_like(acc_ref)

    a = jnp.maximum(_dot(h_ref[...], w1_ref[...]), 0.0)
    acc_ref[...] += _dot((a * a).astype(BF16), w2_ref[...])

    @pl.when(k == pl.num_programs(2) - 1)
    def _():
        res_gate = _select_mod(ml_ref, mc_ref, 5, is_ctx)
        out_ref[...] = _layer_norm(ALPHA * x_ref[...] + res_gate * acc_ref[...], lg_ref[...], lb_ref[...])


def _mlp(xs, modtab, layer, w1, w2, ln_g, ln_b, n_lat):
    b, rows, d = xs.shape
    dff = w1.shape[1]
    tm = _row_tile(rows)
    tf = 512
    lat, ctx = _mod_specs(layer, b)
    return pl.pallas_call(
        functools.partial(_mlp_kernel, tm=tm, n_lat=n_lat),
        grid=(b, rows // tm, dff // tf),
        in_specs=[pl.BlockSpec((None, tm, d), lambda bi, j, k: (bi, j, 0)),
                  pl.BlockSpec((None, None, 6, d), lat),
                  pl.BlockSpec((None, None, 6, d), ctx),
                  pl.BlockSpec((d, tf), lambda bi, j, k: (0, k)),
                  pl.BlockSpec((tf, d), lambda bi, j, k: (k, 0)),
                  pl.BlockSpec((1, d), lambda bi, j, k: (0, 0)),
                  pl.BlockSpec((1, d), lambda bi, j, k: (0, 0))],
        out_specs=pl.BlockSpec((None, tm, d), lambda bi, j, k: (bi, j, 0)),
        out_shape=jax.ShapeDtypeStruct((b, rows, d), F32),
        scratch_shapes=[pltpu.VMEM((tm, d), BF16), pltpu.VMEM((tm, d), F32)],
        compiler_params=_params(("parallel", "parallel", "arbitrary")),
        name="mlp",
    )(xs, modtab, modtab, w1, w2, ln_g.reshape(1, d), ln_b.reshape(1, d))


def _chunk_orders(s, n_lat_chunks, n_ctx_chunks):
    total = n_lat_chunks + n_ctx_chunks
    fwd = jnp.where(s < n_ctx_chunks, s + n_lat_chunks, s - n_ctx_chunks)
    bwd = jnp.where(s < n_ctx_chunks, total - 1 - s, n_lat_chunks + n_ctx_chunks - 1 - s)
    return fwd, bwd


def _cumsum_mats():
    r = lax.broadcasted_iota(jnp.int32, (CHUNK, CHUNK), 0)
    c = lax.broadcasted_iota(jnp.int32, (CHUNK, CHUNK), 1)
    return (c <= r).astype(F32), (c >= r).astype(F32)


def _chunk_cumsums(val, fwd_cols, bwd_cols, rows):
    lower, upper = _cumsum_mats()
    out = []
    for r0 in range(0, rows, CHUNK):
        v = val[r0:r0 + CHUNK]
        cf = jnp.dot(lower, v, precision=HIGHEST, preferred_element_type=F32)
        cb = jnp.dot(upper, v, precision=HIGHEST, preferred_element_type=F32)
        out.append(jnp.where(fwd_cols, cf, jnp.where(bwd_cols, cb, v)))
    return jnp.concatenate(out, axis=0)


def _pick_col(tile, idx):
    lane = lax.broadcasted_iota(jnp.int32, tile.shape, 1)
    return jnp.sum(jnp.where(lane == idx, tile, 0.0), axis=1, keepdims=True)


def _pick_row(tile, idx):
    sub = lax.broadcasted_iota(jnp.int32, tile.shape, 0)
    return jnp.sum(jnp.where(sub == idx, tile, 0.0), axis=0, keepdims=True)


def _tri_masks(direction):
    r = lax.broadcasted_iota(jnp.int32, (CHUNK, CHUNK), 0)
    c = lax.broadcasted_iota(jnp.int32, (CHUNK, CHUNK), 1)
    if direction == 0:
        return c <= r, c < r
    return c >= r, c > r


def _unit_tri_inverse(a):
    r = lax.broadcasted_iota(jnp.int32, (CHUNK, CHUNK), 0)
    c = lax.broadcasted_iota(jnp.int32, (CHUNK, CHUNK), 1)
    differ = r ^ c
    t = jnp.where(r == c, 1.0, 0.0) - jnp.where(differ < 2, a, 0.0)
    size = 2
    while size < CHUNK:
        e = jnp.where((differ >= size) & (differ < 2 * size), a, 0.0)
        t = t - _dot(t, _dot(e, t))
        size *= 2
    return t


def _gdn_gate_kernel(gp_ref, alog_ref, dtb_ref, o_ref, *, rows):
    x = gp_ref[...]
    lane = lax.broadcasted_iota(jnp.int32, (1, LANES), 1)
    live = lane < 4 * H_A
    is_decay = live & ((lane & (2 * H_A - 1)) < H_A)
    g = -jnp.exp(alog_ref[...]) * _softplus(x + dtb_ref[...])
    val = jnp.where(is_decay, g, _sigmoid(x))
    o_ref[...] = _chunk_cumsums(val, is_decay & (lane < 2 * H_A), is_decay & (lane >= 2 * H_A), rows)


def _gdn_gates(gp, a_log, dt_bias):
    b, nt, _ = gp.shape
    rows = 256
    pad = lambda t: jnp.pad(jnp.stack([t[0], jnp.zeros_like(t[0]), t[1], jnp.zeros_like(t[1])]).reshape(1, -1),
                            ((0, 0), (0, LANES - 4 * H_A)))
    return pl.pallas_call(
        functools.partial(_gdn_gate_kernel, rows=rows),
        grid=(b, nt // rows),
        in_specs=[pl.BlockSpec((None, rows, LANES), lambda bi, j: (bi, j, 0)),
                  pl.BlockSpec((1, LANES), lambda bi, j: (0, 0)),
                  pl.BlockSpec((1, LANES), lambda bi, j: (0, 0))],
        out_specs=pl.BlockSpec((None, rows, LANES), lambda bi, j: (bi, j, 0)),
        out_shape=jax.ShapeDtypeStruct((b, nt, LANES), F32),
        compiler_params=_params(("parallel", "parallel")),
        name="gdn_gates",
    )(gp, pad(a_log.astype(F32)), pad(dt_bias.astype(F32)))


def _gdn_conv_kernel(p_ref, w_ref, o_ref, *, n_lat, nt):
    cb = pl.program_id(1)
    x = p_ref[...]
    row = lax.broadcasted_iota(jnp.int32, (nt, 1), 0)
    seg_lo = jnp.where(row >= n_lat, n_lat, 0)
    seg_hi = jnp.where(row >= n_lat, nt, n_lat)
    half = CONV_K // 2
    acc = x * w_ref[half:half + 1, :]
    for s in range(-half, half + 1):
        if s == 0:
            continue
        shifted = pltpu.roll(x, (-s) % nt, axis=0)
        valid = (row + s >= seg_lo) & (row + s < seg_hi)
        acc = acc + jnp.where(valid, shifted, 0.0) * w_ref[half + s:half + s + 1, :]
    y = acc * _sigmoid(acc)
    inv = lax.rsqrt(jnp.sum(y * y, axis=-1, keepdims=True) + EPS)
    factor = jnp.where(cb < H_A, inv * (DK_A ** -0.5), jnp.where(cb < 2 * H_A, inv, 1.0))
    o_ref[...] = y * factor


def _gdn_conv(p, conv_w, n_lat):
    b, nt, _ = p.shape
    nblk = 3 * H_A
    return pl.pallas_call(
        functools.partial(_gdn_conv_kernel, n_lat=n_lat, nt=nt),
        grid=(b, nblk),
        in_specs=[pl.BlockSpec((None, nt, DK_A), lambda bi, cb: (bi, 0, cb)),
                  pl.BlockSpec((CONV_K, DK_A), lambda bi, cb: (0, cb))],
        out_specs=pl.BlockSpec((None, nt, DK_A), lambda bi, cb: (bi, 0, cb)),
        out_shape=jax.ShapeDtypeStruct((b, nt, nblk * DK_A), F32),
        compiler_params=_params(("parallel", "parallel")),
        name="gdn_conv",
    )(p, conv_w.astype(F32))


def _gdn_intra_kernel(q_ref, k_ref, v_ref, g_ref, u_ref, w_ref, qg_ref, kd_ref, qk_ref, gl_ref, *, rows):
    h = pl.program_id(1)
    for pair in range(rows // (2 * CHUNK)):
        p0 = pair * 2 * CHUNK
        gates = g_ref[p0:p0 + 2 * CHUNK, :]
        gates_t = gates.T
        for sub in range(2):
            r0 = p0 + sub * CHUNK
            ci = r0 // CHUNK
            q = q_ref[r0:r0 + CHUNK, :]
            k = k_ref[r0:r0 + CHUNK, :]
            v = v_ref[r0:r0 + CHUNK, :]
            kk = _dot_nt(k, k)
            qk = _dot_nt(q, k)
            tile = gates[sub * CHUNK:(sub + 1) * CHUNK, :]
            for d in range(2):
                tri, strict = _tri_masks(d)
                gcol = _pick_col(tile, 2 * H_A * d + h)
                beta = _pick_col(tile, 2 * H_A * d + H_A + h)
                grow = _pick_row(gates_t, 2 * H_A * d + h)[:, sub * CHUNK:(sub + 1) * CHUNK]
                decay = jnp.where(tri, jnp.exp(jnp.where(tri, gcol - grow, 0.0)), 0.0)
                a = jnp.where(strict, kk * beta * decay, 0.0)
                t_inv = _unit_tri_inverse(a)
                eg = jnp.exp(gcol)
                kb = k * beta
                uw = _dot(t_inv, jnp.concatenate([v * beta, kb * eg], axis=1))
                last = CHUNK - 1 if d == 0 else 0
                g_last = gcol[last:last + 1, :]
                u_ref[d, r0:r0 + CHUNK, :] = uw[:, :DK_A]
                w_ref[d, r0:r0 + CHUNK, :] = uw[:, DK_A:]
                qg_ref[d, r0:r0 + CHUNK, :] = q * eg
                kd_ref[d, r0:r0 + CHUNK, :] = k * jnp.exp(g_last - gcol)
                qk_ref[r0:r0 + CHUNK, d * CHUNK:(d + 1) * CHUNK] = jnp.where(tri, qk * decay, 0.0)
                gl_ref[d, ci] = jnp.broadcast_to(jnp.exp(g_last), (1, LANES))


def _gdn_intra(qkv, gates):
    b, nt, _ = qkv.shape
    rows = 256
    nch = nt // CHUNK
    big = jax.ShapeDtypeStruct((b, H_A, 2, nt, DK_A), F32)
    big_spec = pl.BlockSpec((None, None, 2, rows, DK_A), lambda bi, h, j: (bi, h, 0, j, 0))
    col = lambda off: pl.BlockSpec((None, rows, DK_A), lambda bi, h, j: (bi, j, off + h))
    return pl.pallas_call(
        functools.partial(_gdn_intra_kernel, rows=rows),
        grid=(b, H_A, nt // rows),
        in_specs=[col(0), col(H_A), col(2 * H_A),
                  pl.BlockSpec((None, rows, LANES), lambda bi, h, j: (bi, j, 0))],
        out_specs=[big_spec, big_spec, big_spec, big_spec,
                   pl.BlockSpec((None, None, rows, 2 * CHUNK), lambda bi, h, j: (bi, h, j, 0)),
                   pl.BlockSpec((None, None, 2, rows // CHUNK, 1, LANES), lambda bi, h, j: (bi, h, 0, j, 0, 0))],
        out_shape=[big, big, big, big,
                   jax.ShapeDtypeStruct((b, H_A, nt, 2 * CHUNK), F32),
                   jax.ShapeDtypeStruct((b, H_A, 2, nch, 1, LANES), F32)],
        compiler_params=_params(("parallel", "parallel", "parallel")),
        name="gdn_intra",
    )(qkv, qkv, qkv, gates)


def _gdn_scan_kernel(u_ref, w_ref, qg_ref, kd_ref, qk_ref, gl_ref, o_ref, s_ref, *, n_lat_chunks, n_ctx_chunks):
    s_ref[...] = jnp.zeros_like(s_ref)
    o_ref[...] = jnp.zeros_like(o_ref)

    def step(s, carry):
        order = _chunk_orders(s, n_lat_chunks, n_ctx_chunks)
        for d in range(2):
            c = order[d]
            r0 = pl.multiple_of(c * CHUNK, CHUNK)
            rows = pl.ds(r0, CHUNK)
            state = s_ref[d]
            wq = _dot(jnp.concatenate([w_ref[d, rows, :], qg_ref[d, rows, :]], axis=0), state)
            v_new = u_ref[d, rows, :] - wq[:CHUNK]
            qk = qk_ref[rows, :][:, d * CHUNK:(d + 1) * CHUNK]
            o_ref[rows, :] += wq[CHUNK:] + _dot(qk, v_new)
            s_ref[d] = state * gl_ref[d, c] + _dot_tn(kd_ref[d, rows, :], v_new)
        return carry

    lax.fori_loop(0, n_lat_chunks + n_ctx_chunks, step, 0)


def _gdn_scan(u, w, qg, kd, qk, gl, n_lat):
    b, h, _, nt, dk = u.shape
    nch = nt // CHUNK
    big_spec = pl.BlockSpec((None, None, 2, nt, dk), lambda bi, hi: (bi, hi, 0, 0, 0))
    return pl.pallas_call(
        functools.partial(_gdn_scan_kernel, n_lat_chunks=n_lat // CHUNK, n_ctx_chunks=(nt - n_lat) // CHUNK),
        grid=(b, h),
        in_specs=[big_spec, big_spec, big_spec, big_spec,
                  pl.BlockSpec((None, None, nt, 2 * CHUNK), lambda bi, hi: (bi, hi, 0, 0)),
                  pl.BlockSpec((None, None, 2, nch, 1, LANES), lambda bi, hi: (bi, hi, 0, 0, 0, 0))],
        out_specs=pl.BlockSpec((None, nt, dk), lambda bi, hi: (bi, 0, hi)),
        out_shape=jax.ShapeDtypeStruct((b, nt, h * dk), F32),
        scratch_shapes=[pltpu.VMEM((2, dk, dk), F32)],
        compiler_params=_params(("parallel", "parallel")),
        name="gdn_scan",
    )(u, w, qg, kd, qk, gl)


def _gated_deltanet(xs, modtab, layer, n_lat, w_in, conv_w, a_log, dt_bias):
    key = H_A * DK_A
    w_main = w_in[:, :4 * key].astype(BF16)
    w_gate = jnp.pad(w_in[:, 4 * key:], ((0, 0), (0, LANES - 4 * H_A))).astype(BF16)
    p = _inproj(xs, modtab, layer, w_main, n_lat)
    gp = _inproj(xs, modtab, layer, w_gate, n_lat)
    gates = _gdn_gates(gp, a_log, dt_bias)
    qkv = _gdn_conv(p, conv_w, n_lat)
    u, w, qg, kd, qk, gl = _gdn_intra(qkv, gates)
    o = _gdn_scan(u, w, qg, kd, qk, gl, n_lat)
    return o, p


def _rope_tables(n_lat, nt):
    rows = n_lat // GRID_W
    r = jnp.repeat(jnp.arange(rows, dtype=F32), GRID_W)
    col = jnp.tile(jnp.arange(GRID_W, dtype=F32), rows)
    half = DH_B // 2
    inv = ROPE_BASE ** (-jnp.arange(0, half, 2, dtype=F32) / half)
    ang_r, ang_c = r[:, None] * inv, col[:, None] * inv
    cos = jnp.concatenate([jnp.cos(ang_r)] * 2 + [jnp.cos(ang_c)] * 2, axis=-1)
    sin = jnp.concatenate([-jnp.sin(ang_r), jnp.sin(ang_r), -jnp.sin(ang_c), jnp.sin(ang_c)], axis=-1)
    cos = jnp.concatenate([jnp.tile(cos, (1, LANES // DH_B)), jnp.ones((nt - n_lat, LANES), F32)], axis=0)
    sin = jnp.concatenate([jnp.tile(sin, (1, LANES // DH_B)), jnp.zeros((nt - n_lat, LANES), F32)], axis=0)
    return cos, sin


def _attn_kernel(q_ref, k_ref, v_ref, lam_ref, o_ref, *, tq, n_lat, nt, lam_init):
    j = pl.program_id(2)
    lam = lam_ref[...]
    lam_full = (jnp.exp(jnp.sum(lam[0:1] * lam[1:2], axis=-1, keepdims=True))
                - jnp.exp(jnp.sum(lam[2:3] * lam[3:4], axis=-1, keepdims=True)) + lam_init)
    q = q_ref[...] * (DH_B ** -0.5)
    lane = lax.broadcasted_iota(jnp.int32, (1, 2 * DH_B), 1)
    q0 = jnp.where(lane < DH_B, q, 0.0)
    q1 = jnp.where(lane >= DH_B, q, 0.0)

    def attend(k, v):
        def softmax(qm):
            s = _dot_nt(qm, k)
            e = jnp.exp(s - jnp.max(s, axis=-1, keepdims=True))
            return e, 1.0 / jnp.sum(e, axis=-1, keepdims=True)
        e0, r0 = softmax(q0)
        e1, r1 = softmax(q1)
        o_ref[...] = _dot(e0 * r0 - e1 * (lam_full * r1), v)

    @pl.when(j * tq < n_lat)
    def _():
        attend(k_ref[...], v_ref[...])

    @pl.when(j * tq >= n_lat)
    def _():
        attend(k_ref[n_lat:nt, :], v_ref[n_lat:nt, :])


def _diff_attention(p, lam, n_lat, lam_init):
    b, nt, _ = p.shape
    tq = 256
    dv = 2 * DH_B
    return pl.pallas_call(
        functools.partial(_attn_kernel, tq=tq, n_lat=n_lat, nt=nt, lam_init=lam_init),
        grid=(b, H_B, nt // tq),
        in_specs=[pl.BlockSpec((None, tq, dv), lambda bi, h, j: (bi, j, h)),
                  pl.BlockSpec((None, nt, dv), lambda bi, h, j: (bi, 0, H_B + h)),
                  pl.BlockSpec((None, nt, dv), lambda bi, h, j: (bi, 0, 2 * H_B + h)),
                  pl.BlockSpec((4, DH_B), lambda bi, h, j: (0, 0))],
        out_specs=pl.BlockSpec((None, tq, dv), lambda bi, h, j: (bi, j, h)),
        out_shape=jax.ShapeDtypeStruct((b, nt, H_B * dv), F32),
        compiler_params=_params(("parallel", "parallel", "arbitrary")),
        name="diff_attn",
    )(p, p, p, lam.astype(F32))


def _mlstm_gate_kernel(gp_ref, bias_ref, o_ref, *, rows):
    x = gp_ref[...] + bias_ref[...]
    lane = lax.broadcasted_iota(jnp.int32, (1, LANES), 1)
    live = lane < 4 * H_C
    is_forget = live & ((lane & H_C) != 0)
    val = jnp.where(is_forget, -_softplus(-x), x)
    o_ref[...] = _chunk_cumsums(val, is_forget & (lane < 2 * H_C), is_forget & (lane >= 2 * H_C), rows)


def _mlstm_gates(gp, gate_bias):
    b, nt, _ = gp.shape
    rows = 256
    bias = jnp.pad(gate_bias.astype(F32).reshape(1, -1), ((0, 0), (0, LANES - 4 * H_C)))
    return pl.pallas_call(
        functools.partial(_mlstm_gate_kernel, rows=rows),
        grid=(b, nt // rows),
        in_specs=[pl.BlockSpec((None, rows, LANES), lambda bi, j: (bi, j, 0)),
                  pl.BlockSpec((1, LANES), lambda bi, j: (0, 0))],
        out_specs=pl.BlockSpec((None, rows, LANES), lambda bi, j: (bi, j, 0)),
        out_shape=jax.ShapeDtypeStruct((b, nt, LANES), F32),
        compiler_params=_params(("parallel", "parallel")),
        name="mlstm_gates",
    )(gp, bias)


def _mlstm_intra_kernel(q_ref, k_ref, v_ref, g_ref, num_ref, ke_ref, st_ref, *, rows):
    h = pl.program_id(1)
    lane = lax.broadcasted_iota(jnp.int32, (CHUNK, LANES), 1)
    for pair in range(rows // (2 * CHUNK)):
        p0 = pair * 2 * CHUNK
        gates = g_ref[p0:p0 + 2 * CHUNK, :]
        gates_t = gates.T
        for sub in range(2):
            r0 = p0 + sub * CHUNK
            q = q_ref[r0:r0 + CHUNK, :]
            k = k_ref[r0:r0 + CHUNK, :] * (DQK_C ** -0.5)
            v = v_ref[r0:r0 + CHUNK, :]
            qk = _dot_nt(q, k)
            tile = gates[sub * CHUNK:(sub + 1) * CHUNK, :]
            for d in range(2):
                tri, _ = _tri_masks(d)
                li_idx = 2 * d * H_C + h
                bc_idx = (2 * d + 1) * H_C + h
                bcol = _pick_col(tile, bc_idx)
                licol = _pick_col(tile, li_idx)
                brow = _pick_row(gates_t, bc_idx)[:, sub * CHUNK:(sub + 1) * CHUNK]
                lirow = _pick_row(gates_t, li_idx)[:, sub * CHUNK:(sub + 1) * CHUNK]
                dmat = jnp.where(tri, bcol - brow + lirow, -jnp.inf)
                m_intra = jnp.max(dmat, axis=-1, keepdims=True)
                wts = jnp.exp(dmat - m_intra) * qk
                last = CHUNK - 1 if d == 0 else 0
                b_last = bcol[last:last + 1, :]
                m_end = m_intra[last:last + 1, :]
                num_ref[d, r0:r0 + CHUNK, :] = _dot(wts, v)
                ke_ref[d, r0:r0 + CHUNK, :] = k * jnp.exp(b_last - bcol + licol - m_end)
                den = jnp.sum(wts, axis=-1, keepdims=True)
                stats = jnp.where(lane == 0, bcol, jnp.where(lane == 1, m_intra, jnp.where(
                    lane == 2, den, jnp.where(lane == 3, m_end, jnp.where(lane == 4, b_last, 0.0)))))
                st_ref[d, r0:r0 + CHUNK, :] = stats


def _mlstm_intra(p, gates):
    b, nt, _ = p.shape
    rows = 256
    qk_w = H_C * DQK_C
    q_spec = pl.BlockSpec((None, rows, DQK_C), lambda bi, h, j: (bi, j, h))
    k_spec = pl.BlockSpec((None, rows, DQK_C), lambda bi, h, j: (bi, j, H_C + h))
    v_spec = pl.BlockSpec((None, rows, DV_C), lambda bi, h, j: (bi, j, 2 * qk_w // DV_C + h))
    out = lambda w: (jax.ShapeDtypeStruct((b, H_C, 2, nt, w), F32),
                     pl.BlockSpec((None, None, 2, rows, w), lambda bi, h, j: (bi, h, 0, j, 0)))
    (num_s, num_b), (ke_s, ke_b), (st_s, st_b) = out(DV_C), out(DQK_C), out(LANES)
    return pl.pallas_call(
        functools.partial(_mlstm_intra_kernel, rows=rows),
        grid=(b, H_C, nt // rows),
        in_specs=[q_spec, k_spec, v_spec, pl.BlockSpec((None, rows, LANES), lambda bi, h, j: (bi, j, 0))],
        out_specs=[num_b, ke_b, st_b],
        out_shape=[num_s, ke_s, st_s],
        compiler_params=_params(("parallel", "parallel", "parallel")),
        name="mlstm_intra",
    )(p, p, p, gates)


def _mlstm_scan_kernel(q_ref, v_ref, num_ref, ke_ref, st_ref, o_ref, c_ref, n_ref, m_ref, *, n_lat_chunks, n_ctx_chunks):
    c_ref[...] = jnp.zeros_like(c_ref)
    n_ref[...] = jnp.zeros_like(n_ref)
    m_ref[...] = jnp.zeros_like(m_ref)
    o_ref[...] = jnp.zeros_like(o_ref)

    def step(s, carry):
        order = _chunk_orders(s, n_lat_chunks, n_ctx_chunks)
        for d in range(2):
            r0 = pl.multiple_of(order[d] * CHUNK, CHUNK)
            rows = pl.ds(r0, CHUNK)
            q = q_ref[rows, :]
            v = v_ref[rows, :]
            ke = ke_ref[d, rows, :]
            st = st_ref[d, rows, :]
            bc, mi, den_c = st[:, 0:1], st[:, 1:2], st[:, 2:3]
            me, bl = st[0:1, 3:4], st[0:1, 4:5]
            c_st = c_ref[d]
            n_st = n_ref[d]
            m0 = m_ref[d][:, 0:1]
            m_t = jnp.maximum(bc + m0, mi)
            s_inter = jnp.exp(bc + m0 - m_t)
            s_intra = jnp.exp(mi - m_t)
            num = s_inter * _dot(q, c_st) + s_intra * num_ref[d, rows, :]
            den = s_inter * jnp.sum(q * n_st, axis=-1, keepdims=True) + s_intra * den_c
            o_ref[rows, :] += num * (1.0 / jnp.maximum(jnp.abs(den), jnp.exp(-m_t)))
            m_new = jnp.maximum(bl + m0, me)
            s_old = jnp.exp(bl + m0 - m_new)
            s_new = jnp.exp(me - m_new)
            c_ref[d] = s_old * c_st + s_new * _dot_tn(ke, v)
            n_ref[d] = s_old * n_st + s_new * jnp.sum(ke, axis=0, keepdims=True)
            m_ref[d] = jnp.broadcast_to(m_new, (1, LANES))
        return carry

    lax.fori_loop(0, n_lat_chunks + n_ctx_chunks, step, 0)


def _mlstm_scan(p, num, ke, st, n_lat):
    b, nt, _ = p.shape
    qk_w = H_C * DQK_C
    per_dir = lambda w: pl.BlockSpec((None, None, 2, nt, w), lambda bi, h: (bi, h, 0, 0, 0))
    return pl.pallas_call(
        functools.partial(_mlstm_scan_kernel, n_lat_chunks=n_lat // CHUNK, n_ctx_chunks=(nt - n_lat) // CHUNK),
        grid=(b, H_C),
        in_specs=[pl.BlockSpec((None, nt, DQK_C), lambda bi, h: (bi, 0, h)),
                  pl.BlockSpec((None, nt, DV_C), lambda bi, h: (bi, 0, 2 * qk_w // DV_C + h)),
                  per_dir(DV_C), per_dir(DQK_C), per_dir(LANES)],
        out_specs=pl.BlockSpec((None, nt, DV_C), lambda bi, h: (bi, 0, h)),
        out_shape=jax.ShapeDtypeStruct((b, nt, H_C * DV_C), F32),
        scratch_shapes=[pltpu.VMEM((2, DQK_C, DV_C), F32), pltpu.VMEM((2, 1, DQK_C), F32),
                        pltpu.VMEM((2, 1, LANES), F32)],
        compiler_params=_params(("parallel", "parallel")),
        name="mlstm_scan",
    )(p, p, num, ke, st)


def _mlstm(xs, modtab, layer, n_lat, w_in, gate_bias):
    main = 2 * H_C * DQK_C + 2 * H_C * DV_C
    w_main = w_in[:, :main].astype(BF16)
    w_gate = jnp.pad(w_in[:, main:], ((0, 0), (0, LANES - 4 * H_C))).astype(BF16)
    p = _inproj(xs, modtab, layer, w_main, n_lat)
    gp = _inproj(xs, modtab, layer, w_gate, n_lat)
    gates = _mlstm_gates(gp, gate_bias)
    num, ke, st = _mlstm_intra(p, gates)
    return _mlstm_scan(p, num, ke, st, n_lat), p


def kernel(x, c, ctx, c_ctx, w_mod, b_mod, ln_g, ln_b, w_in_a, conv_a, a_log_a, dt_bias_a, norm_a, w_out_a,
           w_in_b, lam_b, subln_b, w_out_b, w_in_c, gate_bias_c, norm_c, w_out_c, w1, w2):
    b, n_lat, d = x.shape
    nt = n_lat + ctx.shape[1]
    depth = w_mod.shape[0]
    xs = jnp.concatenate([x, ctx], axis=1)
    cond_rows = -(-(b + 1) // 8) * 8
    cc = jnp.concatenate([c, c_ctx[None, :], jnp.zeros((cond_rows - b - 1, d), F32)], axis=0)
    modtab = _mod_tables(cc, w_mod, b_mod)
    rope = _rope_tables(n_lat, nt)

    for i in range(depth):
        kind, j = i % N_MIXERS, i // N_MIXERS
        rows = nt if i < depth - 1 else n_lat
        if kind == 0:
            o, p = _gated_deltanet(xs, modtab, i, n_lat, w_in_a[j], conv_a[j], a_log_a[j], dt_bias_a[j])
            xs_new = _post(xs, modtab, i, o, p, 3, jnp.tile(norm_a[j], H_A), w_out_a[j].astype(BF16),
                           ln_g[i, 0], ln_b[i, 0], n_lat, rows, DK_A, "silu")
        elif kind == 1:
            lam_init = 0.8 - 0.6 * math.exp(-0.3 * i)
            p = _inproj(xs, modtab, i, w_in_b[j].astype(BF16), n_lat, rope=rope, rope_blocks=2)
            o = _diff_attention(p, lam_b[j], n_lat, lam_init)
            xs_new = _post(xs, modtab, i, o, None, 0, jnp.tile(subln_b[j], H_B), w_out_b[j].astype(BF16),
                           ln_g[i, 0], ln_b[i, 0], n_lat, rows, 2 * DH_B, None, out_scale=1.0 - lam_init)
        else:
            o, p = _mlstm(xs, modtab, i, n_lat, w_in_c[j], gate_bias_c[j])
            xs_new = _post(xs, modtab, i, o, p, 2, norm_c[j], w_out_c[j].astype(BF16),
                           ln_g[i, 0], ln_b[i, 0], n_lat, rows, DV_C, "sigmoid")
        xs = _mlp(xs_new, modtab, i, w1[i].astype(BF16), w2[i].astype(BF16), ln_g[i, 1], ln_b[i, 1], n_lat)
    return xs
```

```python
import functools
import math

import jax
import jax.numpy as jnp
from jax import lax
from jax.experimental import pallas as pl
from jax.experimental.pallas import tpu as pltpu

F32 = jnp.float32
BF16 = jnp.bfloat16
HIGHEST = lax.Precision.HIGHEST

DEPTH = 4
N_MIXERS = 3
ALPHA = (2.0 * DEPTH) ** 0.25
EPS = 1e-6
LN_EPS = 1e-5
GRID_W = 64
ROPE_BASE = 10000.0
LOG2_E = math.log2(math.e)
CHUNK = 64
SCAN_SEG = 256
ATTN_QUERY_ROWS = 128
ATTN_KEY_BLOCK = 256
LANES = 128

H_A, DK_A = 8, 128
CONV_K = 5
H_B, DH_B = 8, 64
H_C, DQK_C, DV_C = 4, 128, 256

VMEM_LIMIT = 56 * 1024 * 1024


def _params(sem, vmem=VMEM_LIMIT):
    return pltpu.CompilerParams(dimension_semantics=sem, vmem_limit_bytes=vmem)


def _sigmoid(x):
    return 1.0 / (1.0 + jnp.exp(-x))


def _softplus(x):
    return jnp.maximum(x, 0.0) + jnp.log1p(jnp.exp(-jnp.abs(x)))


def _dot(a, b):
    return jnp.dot(a, b, preferred_element_type=F32)


def _dot_nt(a, b):
    return lax.dot_general(a, b, (((1,), (1,)), ((), ())), preferred_element_type=F32)


def _dot_tn(a, b):
    return lax.dot_general(a, b, (((0,), (0,)), ((), ())), preferred_element_type=F32)


def _row_tile(nt):
    for tm in (768, 512, 256):
        if nt % tm == 0:
            return tm
    raise ValueError(f"sequence length {nt} must be a multiple of 256")


def _mod_kernel(c_ref, w_ref, b_ref, o_ref):
    cc = c_ref[...]
    s = cc * _sigmoid(cc)
    o_ref[...] = jnp.dot(s, w_ref[...], precision=HIGHEST, preferred_element_type=F32) + b_ref[...]


def _mod_tables(cc, w_mod, b_mod):
    r, d = cc.shape
    depth, _, f = w_mod.shape
    tn = 1536
    out = pl.pallas_call(
        _mod_kernel,
        grid=(depth, f // tn),
        in_specs=[pl.BlockSpec((r, d), lambda l, n: (0, 0)),
                  pl.BlockSpec((None, d, tn), lambda l, n: (l, 0, n)),
                  pl.BlockSpec((None, 1, tn), lambda l, n: (l, 0, n))],
        out_specs=pl.BlockSpec((None, r, tn), lambda l, n: (l, 0, n)),
        out_shape=jax.ShapeDtypeStruct((depth, r, f), F32),
        compiler_params=_params(("parallel", "parallel")),
        name="mod_tables",
    )(cc, w_mod, b_mod.reshape(depth, 1, f))
    return out.reshape(depth, r, 6, d)


def _mod_specs(layer, ctx_row):
    def lat(b, *_):
        return (layer, b, 0, 0)

    def ctx(b, *_):
        return (layer, ctx_row, 0, 0)
    return lat, ctx


def _select_mod(ml_ref, mc_ref, idx, is_ctx):
    return jnp.where(is_ctx, mc_ref[idx:idx + 1, :], ml_ref[idx:idx + 1, :])


def _inproj_kernel(x_ref, ml_ref, mc_ref, w_ref, *rest, tm, n_lat, rope_blocks):
    if rope_blocks:
        cos_ref, sin_ref, o_ref, h_ref = rest
    else:
        o_ref, h_ref = rest
    j = pl.program_id(1)
    n = pl.program_id(2)

    @pl.when(n == 0)
    def _():
        row = j * tm + lax.broadcasted_iota(jnp.int32, (tm, 1), 0)
        is_ctx = row >= n_lat
        shift = _select_mod(ml_ref, mc_ref, 0, is_ctx)
        scale = _select_mod(ml_ref, mc_ref, 1, is_ctx)
        h_ref[...] = (x_ref[...] * (1.0 + scale) + shift).astype(h_ref.dtype)

    acc = _dot(h_ref[...], w_ref[...])
    if not rope_blocks:
        o_ref[...] = acc.astype(o_ref.dtype)
        return

    @pl.when(n >= rope_blocks)
    def _():
        o_ref[...] = acc.astype(o_ref.dtype)

    @pl.when(n < rope_blocks)
    def _():
        q_scale = jnp.where(n == 0, (DH_B ** -0.5) * LOG2_E, 1.0)
        cos = cos_ref[...] * q_scale
        sin = sin_ref[...] * q_scale
        lane = lax.broadcasted_iota(jnp.int32, (1, LANES), 1)
        first_half = (lane & 31) < 16
        for i in range(acc.shape[1] // LANES):
            xb = acc[:, i * LANES:(i + 1) * LANES]
            partner = jnp.where(first_half, pltpu.roll(xb, LANES - 16, axis=1), pltpu.roll(xb, 16, axis=1))
            o_ref[:, i * LANES:(i + 1) * LANES] = (xb * cos + partner * sin).astype(o_ref.dtype)


def _inproj(xs, modtab, layer, w, n_lat, rope=None, rope_blocks=0, out_dtype=F32):
    b, nt, d = xs.shape
    f = w.shape[1]
    tm = _row_tile(nt)
    tn = min(f, 1024)
    lat, ctx = _mod_specs(layer, b)
    in_specs = [pl.BlockSpec((None, tm, d), lambda bi, j, n: (bi, j, 0)),
                pl.BlockSpec((None, None, 6, d), lat),
                pl.BlockSpec((None, None, 6, d), ctx),
                pl.BlockSpec((d, tn), lambda bi, j, n: (0, n))]
    args = [xs, modtab, modtab, w]
    if rope_blocks:
        in_specs += [pl.BlockSpec((tm, LANES), lambda bi, j, n: (j, 0))] * 2
        args += list(rope)
    return pl.pallas_call(
        functools.partial(_inproj_kernel, tm=tm, n_lat=n_lat, rope_blocks=rope_blocks),
        grid=(b, nt // tm, f // tn),
        in_specs=in_specs,
        out_specs=pl.BlockSpec((None, tm, tn), lambda bi, j, n: (bi, j, n)),
        out_shape=jax.ShapeDtypeStruct((b, nt, f), out_dtype),
        scratch_shapes=[pltpu.VMEM((tm, d), BF16)],
        compiler_params=_params(("parallel", "parallel", "arbitrary")),
        name="inproj",
    )(*args)


def _layer_norm(v, g, b):
    mu = jnp.mean(v, axis=-1, keepdims=True)
    vc = v - mu
    var = jnp.mean(vc * vc, axis=-1, keepdims=True)
    return vc * lax.rsqrt(var + LN_EPS) * g + b


def _post_kernel(x_ref, ml_ref, mc_ref, o_ref, *rest, tm, n_lat, group, gate, out_scale):
    if gate is None:
        gn_ref, w_ref, lg_ref, lb_ref, out_ref = rest
    else:
        z_ref, gn_ref, w_ref, lg_ref, lb_ref, out_ref = rest
    j = pl.program_id(1)
    row = j * tm + lax.broadcasted_iota(jnp.int32, (tm, 1), 0)
    is_ctx = row >= n_lat
    width = o_ref.shape[-1]
    parts = []
    for g0 in range(0, width, group):
        og = o_ref[:, g0:g0 + group]
        ms = jnp.mean(og * og, axis=-1, keepdims=True)
        yg = og * lax.rsqrt(ms + EPS) * gn_ref[:, g0:g0 + group]
        if gate == "silu":
            z = z_ref[:, g0:g0 + group]
            yg = yg * (z * _sigmoid(z))
        elif gate == "sigmoid":
            yg = yg * _sigmoid(z_ref[:, g0:g0 + group])
        else:
            yg = yg * out_scale
        parts.append(yg.astype(BF16))
    y = _dot(jnp.concatenate(parts, axis=-1), w_ref[...])
    res_gate = _select_mod(ml_ref, mc_ref, 2, is_ctx)
    out_ref[...] = _layer_norm(ALPHA * x_ref[...] + res_gate * y, lg_ref[...], lb_ref[...])


def _post(xs, modtab, layer, o, z_arr, z_block, gnorm, w_out, ln_g, ln_b, n_lat, rows, group, gate, out_scale=1.0):
    b, _, d = xs.shape
    width = o.shape[-1]
    tm = _row_tile(rows)
    lat, ctx = _mod_specs(layer, b)
    in_specs = [pl.BlockSpec((None, tm, d), lambda bi, j: (bi, j, 0)),
                pl.BlockSpec((None, None, 6, d), lat),
                pl.BlockSpec((None, None, 6, d), ctx),
                pl.BlockSpec((None, tm, width), lambda bi, j: (bi, j, 0))]
    args = [xs, modtab, modtab, o]
    if gate is not None:
        in_specs.append(pl.BlockSpec((None, tm, width), lambda bi, j: (bi, j, z_block)))
        args.append(z_arr)
    in_specs += [pl.BlockSpec((1, width), lambda bi, j: (0, 0)),
                 pl.BlockSpec((width, d), lambda bi, j: (0, 0)),
                 pl.BlockSpec((1, d), lambda bi, j: (0, 0)),
                 pl.BlockSpec((1, d), lambda bi, j: (0, 0))]
    args += [gnorm.reshape(1, width), w_out, ln_g.reshape(1, d), ln_b.reshape(1, d)]
    return pl.pallas_call(
        functools.partial(_post_kernel, tm=tm, n_lat=n_lat, group=group, gate=gate, out_scale=out_scale),
        grid=(b, rows // tm),
        in_specs=in_specs,
        out_specs=pl.BlockSpec((None, tm, d), lambda bi, j: (bi, j, 0)),
        out_shape=jax.ShapeDtypeStruct((b, rows, d), F32),
        compiler_params=_params(("parallel", "parallel")),
        name="mixer_out",
    )(*args)


def _mlp_kernel(x_ref, ml_ref, mc_ref, w1_ref, w2_ref, lg_ref, lb_ref, out_ref, h_ref, acc_ref, *, tm, n_lat):
    j = pl.program_id(1)
    k = pl.program_id(2)
    row = j * tm + lax.broadcasted_iota(jnp.int32, (tm, 1), 0)
    is_ctx = row >= n_lat

    @pl.when(k == 0)
    def _():
        shift = _select_mod(ml_ref, mc_ref, 3, is_ctx)
        scale = _select_mod(ml_ref, mc_ref, 4, is_ctx)
        h_ref[...] = (x_ref[...] * (1.0 + scale) + shift).astype(h_ref.dtype)
        acc_ref[...] = jnp.zeros_like(acc_ref)

    a = jnp.maximum(_dot(h_ref[...], w1_ref[...]), 0.0)
    acc_ref[...] += _dot((a * a).astype(BF16), w2_ref[...])

    @pl.when(k == pl.num_programs(2) - 1)
    def _():
        res_gate = _select_mod(ml_ref, mc_ref, 5, is_ctx)
        out_ref[...] = _layer_norm(ALPHA * x_ref[...] + res_gate * acc_ref[...], lg_ref[...], lb_ref[...])


def _mlp(xs, modtab, layer, w1, w2, ln_g, ln_b, n_lat):
    b, rows, d = xs.shape
    dff = w1.shape[1]
    tm = _row_tile(rows)
    tf = 512
    lat, ctx = _mod_specs(layer, b)
    return pl.pallas_call(
        functools.partial(_mlp_kernel, tm=tm, n_lat=n_lat),
        grid=(b, rows // tm, dff // tf),
        in_specs=[pl.BlockSpec((None, tm, d), lambda bi, j, k: (bi, j, 0)),
                  pl.BlockSpec((None, None, 6, d), lat),
                  pl.BlockSpec((None, None, 6, d), ctx),
                  pl.BlockSpec((d, tf), lambda bi, j, k: (0, k)),
                  pl.BlockSpec((tf, d), lambda bi, j, k: (k, 0)),
                  pl.BlockSpec((1, d), lambda bi, j, k: (0, 0)),
                  pl.BlockSpec((1, d), lambda bi, j, k: (0, 0))],
        out_specs=pl.BlockSpec((None, tm, d), lambda bi, j, k: (bi, j, 0)),
        out_shape=jax.ShapeDtypeStruct((b, rows, d), F32),
        scratch_shapes=[pltpu.VMEM((tm, d), BF16), pltpu.VMEM((tm, d), F32)],
        compiler_params=_params(("parallel", "parallel", "arbitrary")),
        name="mlp",
    )(xs, modtab, modtab, w1, w2, ln_g.reshape(1, d), ln_b.reshape(1, d))


def _segment_orders(n_lat_seg, n_ctx_seg):
    def fwd(j):
        return jnp.where(j < n_ctx_seg, j + n_lat_seg, j - n_ctx_seg)

    def bwd(j):
        return n_lat_seg + n_ctx_seg - 1 - j
    return fwd, bwd


def _cumsum_mats():
    r = lax.broadcasted_iota(jnp.int32, (CHUNK, CHUNK), 0)
    c = lax.broadcasted_iota(jnp.int32, (CHUNK, CHUNK), 1)
    return (c <= r).astype(F32), (c >= r).astype(F32)


def _chunk_cumsums(val, fwd_cols, bwd_cols, rows):
    lower, upper = _cumsum_mats()
    out = []
    for r0 in range(0, rows, CHUNK):
        v = val[r0:r0 + CHUNK]
        cf = jnp.dot(lower, v, precision=HIGHEST, preferred_element_type=F32)
        cb = jnp.dot(upper, v, precision=HIGHEST, preferred_element_type=F32)
        out.append(jnp.where(fwd_cols, cf, jnp.where(bwd_cols, cb, v)))
    return jnp.concatenate(out, axis=0)


def _pick_col(tile, idx):
    lane = lax.broadcasted_iota(jnp.int32, tile.shape, 1)
    return jnp.sum(jnp.where(lane == idx, tile, 0.0), axis=1, keepdims=True)


def _pick_row(tile, idx):
    sub = lax.broadcasted_iota(jnp.int32, tile.shape, 0)
    return jnp.sum(jnp.where(sub == idx, tile, 0.0), axis=0, keepdims=True)


def _tri_masks(direction):
    r = lax.broadcasted_iota(jnp.int32, (CHUNK, CHUNK), 0)
    c = lax.broadcasted_iota(jnp.int32, (CHUNK, CHUNK), 1)
    if direction == 0:
        return c <= r, c < r
    return c >= r, c > r


def _unit_tri_inverses(mats):
    r = lax.broadcasted_iota(jnp.int32, (CHUNK, CHUNK), 0)
    c = lax.broadcasted_iota(jnp.int32, (CHUNK, CHUNK), 1)
    differ = r ^ c
    eye = jnp.where(r == c, 1.0, 0.0)
    ts = [eye - jnp.where(differ < 2, a, 0.0) for a in mats]
    size = 2
    while size < CHUNK:
        joins = (differ >= size) & (differ < 2 * size)
        ets = [_dot(jnp.where(joins, a, 0.0), t) for a, t in zip(mats, ts)]
        ts = [t - _dot(t, et) for t, et in zip(ts, ets)]
        size *= 2
    return ts


def _gdn_gate_kernel(gp_ref, alog_ref, dtb_ref, o_ref, *, rows):
    x = gp_ref[...]
    lane = lax.broadcasted_iota(jnp.int32, (1, LANES), 1)
    live = lane < 4 * H_A
    is_decay = live & ((lane & (2 * H_A - 1)) < H_A)
    g = -jnp.exp(alog_ref[...]) * _softplus(x + dtb_ref[...])
    val = jnp.where(is_decay, g, _sigmoid(x))
    o_ref[...] = _chunk_cumsums(val, is_decay & (lane < 2 * H_A), is_decay & (lane >= 2 * H_A), rows)


def _gdn_gates(gp, a_log, dt_bias):
    b, nt, _ = gp.shape
    rows = 256
    pad = lambda t: jnp.pad(jnp.stack([t[0], jnp.zeros_like(t[0]), t[1], jnp.zeros_like(t[1])]).reshape(1, -1),
                            ((0, 0), (0, LANES - 4 * H_A)))
    return pl.pallas_call(
        functools.partial(_gdn_gate_kernel, rows=rows),
        grid=(b, nt // rows),
        in_specs=[pl.BlockSpec((None, rows, LANES), lambda bi, j: (bi, j, 0)),
                  pl.BlockSpec((1, LANES), lambda bi, j: (0, 0)),
                  pl.BlockSpec((1, LANES), lambda bi, j: (0, 0))],
        out_specs=pl.BlockSpec((None, rows, LANES), lambda bi, j: (bi, j, 0)),
        out_shape=jax.ShapeDtypeStruct((b, nt, LANES), F32),
        compiler_params=_params(("parallel", "parallel")),
        name="gdn_gates",
    )(gp, pad(a_log.astype(F32)), pad(dt_bias.astype(F32)))


def _gdn_conv_kernel(p_ref, w_ref, o_ref, *, n_lat, nt):
    cb = pl.program_id(1)
    x = p_ref[...]
    row = lax.broadcasted_iota(jnp.int32, (nt, 1), 0)
    seg_lo = jnp.where(row >= n_lat, n_lat, 0)
    seg_hi = jnp.where(row >= n_lat, nt, n_lat)
    half = CONV_K // 2
    acc = x * w_ref[half:half + 1, :]
    for s in range(-half, half + 1):
        if s == 0:
            continue
        shifted = pltpu.roll(x, (-s) % nt, axis=0)
        valid = (row + s >= seg_lo) & (row + s < seg_hi)
        acc = acc + jnp.where(valid, shifted, 0.0) * w_ref[half + s:half + s + 1, :]
    y = acc * _sigmoid(acc)
    inv = lax.rsqrt(jnp.sum(y * y, axis=-1, keepdims=True) + EPS)
    factor = jnp.where(cb < H_A, inv * (DK_A ** -0.5), jnp.where(cb < 2 * H_A, inv, 1.0))
    o_ref[...] = y * factor


def _gdn_conv(p, conv_w, n_lat):
    b, nt, _ = p.shape
    nblk = 3 * H_A
    return pl.pallas_call(
        functools.partial(_gdn_conv_kernel, n_lat=n_lat, nt=nt),
        grid=(b, nblk),
        in_specs=[pl.BlockSpec((None, nt, DK_A), lambda bi, cb: (bi, 0, cb)),
                  pl.BlockSpec((CONV_K, DK_A), lambda bi, cb: (0, cb))],
        out_specs=pl.BlockSpec((None, nt, DK_A), lambda bi, cb: (bi, 0, cb)),
        out_shape=jax.ShapeDtypeStruct((b, nt, nblk * DK_A), F32),
        compiler_params=_params(("parallel", "parallel")),
        name="gdn_conv",
    )(p, conv_w.astype(F32))


def _gdn_intra_kernel(q_ref, k_ref, v_ref, g_ref, u_ref, w_ref, qg_ref, kd_ref, qk_ref, gl_ref, *, rows):
    h = pl.program_id(1)
    chains = []
    for pair in range(rows // (2 * CHUNK)):
        p0 = pair * 2 * CHUNK
        gates = g_ref[p0:p0 + 2 * CHUNK, :]
        gates_t = gates.T
        for sub in range(2):
            r0 = p0 + sub * CHUNK
            q = q_ref[r0:r0 + CHUNK, :]
            k = k_ref[r0:r0 + CHUNK, :]
            v = v_ref[r0:r0 + CHUNK, :]
            kk = _dot_nt(k, k)
            qk = _dot_nt(q, k)
            tile = gates[sub * CHUNK:(sub + 1) * CHUNK, :]
            for d in range(2):
                tri, strict = _tri_masks(d)
                gcol = _pick_col(tile, 2 * H_A * d + h)
                beta = _pick_col(tile, 2 * H_A * d + H_A + h)
                grow = _pick_row(gates_t, 2 * H_A * d + h)[:, sub * CHUNK:(sub + 1) * CHUNK]
                decay = jnp.where(tri, jnp.exp(jnp.where(tri, gcol - grow, 0.0)), 0.0)
                eg = jnp.exp(gcol)
                last = CHUNK - 1 if d == 0 else 0
                g_last = gcol[last:last + 1, :]
                qg_ref[d, r0:r0 + CHUNK, :] = q * eg
                kd_ref[d, r0:r0 + CHUNK, :] = k * jnp.exp(g_last - gcol)
                qk_ref[r0:r0 + CHUNK, d * CHUNK:(d + 1) * CHUNK] = jnp.where(tri, qk * decay, 0.0)
                gl_ref[d, r0 // CHUNK] = jnp.broadcast_to(jnp.exp(g_last), (1, LANES))
                chains.append((d, r0, jnp.where(strict, kk * beta * decay, 0.0),
                               jnp.concatenate([v * beta, k * (beta * eg)], axis=1)))
    t_invs = _unit_tri_inverses([a for _, _, a, _ in chains])
    for (d, r0, _, rhs), t_inv in zip(chains, t_invs):
        uw = _dot(t_inv, rhs)
        u_ref[d, r0:r0 + CHUNK, :] = uw[:, :DK_A]
        w_ref[d, r0:r0 + CHUNK, :] = uw[:, DK_A:]


def _gdn_intra(qkv, gates):
    b, nt, _ = qkv.shape
    rows = _row_tile(nt)
    nch = nt // CHUNK
    big = jax.ShapeDtypeStruct((b, H_A, 2, nt, DK_A), F32)
    big_spec = pl.BlockSpec((None, None, 2, rows, DK_A), lambda bi, h, j: (bi, h, 0, j, 0))
    col = lambda off: pl.BlockSpec((None, rows, DK_A), lambda bi, h, j: (bi, j, off + h))
    return pl.pallas_call(
        functools.partial(_gdn_intra_kernel, rows=rows),
        grid=(b, H_A, nt // rows),
        in_specs=[col(0), col(H_A), col(2 * H_A),
                  pl.BlockSpec((None, rows, LANES), lambda bi, h, j: (bi, j, 0))],
        out_specs=[big_spec, big_spec, big_spec, big_spec,
                   pl.BlockSpec((None, None, rows, 2 * CHUNK), lambda bi, h, j: (bi, h, j, 0)),
                   pl.BlockSpec((None, None, 2, rows // CHUNK, 1, LANES), lambda bi, h, j: (bi, h, 0, j, 0, 0))],
        out_shape=[big, big, big, big,
                   jax.ShapeDtypeStruct((b, H_A, nt, 2 * CHUNK), F32),
                   jax.ShapeDtypeStruct((b, H_A, 2, nch, 1, LANES), F32)],
        compiler_params=_params(("parallel", "parallel", "parallel")),
        name="gdn_intra",
    )(qkv, qkv, qkv, gates)


def _gdn_scan_kernel(*refs, n_lat_seg, n_ctx_seg):
    dir_refs = (refs[0:6], refs[6:12])
    o_ref, s_ref = refs[12:]
    j = pl.program_id(1)
    fwd_seg, bwd_seg = _segment_orders(n_lat_seg, n_ctx_seg)
    seg_row0 = (pl.multiple_of(fwd_seg(j) * SCAN_SEG, SCAN_SEG), pl.multiple_of(bwd_seg(j) * SCAN_SEG, SCAN_SEG))

    @pl.when(j == 0)
    def _():
        s_ref[...] = jnp.zeros_like(s_ref)
        o_ref[...] = jnp.zeros_like(o_ref)

    nsub = SCAN_SEG // CHUNK
    for sub in range(nsub):
        chains = []
        for d in range(2):
            c = sub if d == 0 else nsub - 1 - sub
            rows = slice(c * CHUNK, (c + 1) * CHUNK)
            _, w_ref, qg_ref, _, _, _ = dir_refs[d]
            for h in range(H_A):
                state = s_ref[d, h]
                wq = _dot(jnp.concatenate([w_ref[h, rows, :], qg_ref[h, rows, :]], axis=0), state)
                chains.append((d, h, c, rows, state, wq))
        for d, h, c, rows, state, wq in chains:
            u_ref, _, _, kd_ref, qk_ref, gl_ref = dir_refs[d]
            v_new = u_ref[h, rows, :] - wq[:CHUNK]
            qk = qk_ref[h, rows, d * CHUNK:(d + 1) * CHUNK]
            out_rows = pl.ds(seg_row0[d] + c * CHUNK, CHUNK)
            o_ref[out_rows, h * DK_A:(h + 1) * DK_A] += wq[CHUNK:] + _dot(qk, v_new)
            s_ref[d, h] = state * gl_ref[h, c] + _dot_tn(kd_ref[h, rows, :], v_new)


def _gdn_scan(u, w, qg, kd, qk, gl, n_lat):
    b, h, _, nt, dk = u.shape
    nsub = SCAN_SEG // CHUNK
    n_lat_seg, n_ctx_seg = n_lat // SCAN_SEG, (nt - n_lat) // SCAN_SEG
    orders = _segment_orders(n_lat_seg, n_ctx_seg)
    in_specs = []
    for d in range(2):
        seg = orders[d]
        big = pl.BlockSpec((None, h, None, SCAN_SEG, dk), lambda bi, j, d=d, seg=seg: (bi, 0, d, seg(j), 0))
        in_specs += [big, big, big, big,
                     pl.BlockSpec((None, h, SCAN_SEG, 2 * CHUNK), lambda bi, j, seg=seg: (bi, 0, seg(j), 0)),
                     pl.BlockSpec((None, h, None, nsub, 1, LANES), lambda bi, j, d=d, seg=seg: (bi, 0, d, seg(j), 0, 0))]
    return pl.pallas_call(
        functools.partial(_gdn_scan_kernel, n_lat_seg=n_lat_seg, n_ctx_seg=n_ctx_seg),
        grid=(b, nt // SCAN_SEG),
        in_specs=in_specs,
        out_specs=pl.BlockSpec((None, nt, h * dk), lambda bi, j: (bi, 0, 0)),
        out_shape=jax.ShapeDtypeStruct((b, nt, h * dk), F32),
        scratch_shapes=[pltpu.VMEM((2, h, dk, dk), F32)],
        compiler_params=_params(("parallel", "arbitrary")),
        name="gdn_scan",
    )(u, w, qg, kd, qk, gl, u, w, qg, kd, qk, gl)


def _gated_deltanet(xs, modtab, layer, n_lat, w_in, conv_w, a_log, dt_bias):
    key = H_A * DK_A
    w_main = w_in[:, :4 * key].astype(BF16)
    w_gate = jnp.pad(w_in[:, 4 * key:], ((0, 0), (0, LANES - 4 * H_A))).astype(BF16)
    p = _inproj(xs, modtab, layer, w_main, n_lat)
    gp = _inproj(xs, modtab, layer, w_gate, n_lat)
    gates = _gdn_gates(gp, a_log, dt_bias)
    qkv = _gdn_conv(p, conv_w, n_lat)
    u, w, qg, kd, qk, gl = _gdn_intra(qkv, gates)
    o = _gdn_scan(u, w, qg, kd, qk, gl, n_lat)
    return o, p


def _rope_tables(n_lat, nt):
    rows = n_lat // GRID_W
    r = jnp.repeat(jnp.arange(rows, dtype=F32), GRID_W)
    col = jnp.tile(jnp.arange(GRID_W, dtype=F32), rows)
    half = DH_B // 2
    inv = ROPE_BASE ** (-jnp.arange(0, half, 2, dtype=F32) / half)
    ang_r, ang_c = r[:, None] * inv, col[:, None] * inv
    cos = jnp.concatenate([jnp.cos(ang_r)] * 2 + [jnp.cos(ang_c)] * 2, axis=-1)
    sin = jnp.concatenate([-jnp.sin(ang_r), jnp.sin(ang_r), -jnp.sin(ang_c), jnp.sin(ang_c)], axis=-1)
    cos = jnp.concatenate([jnp.tile(cos, (1, LANES // DH_B)), jnp.ones((nt - n_lat, LANES), F32)], axis=0)
    sin = jnp.concatenate([jnp.tile(sin, (1, LANES // DH_B)), jnp.zeros((nt - n_lat, LANES), F32)], axis=0)
    return cos, sin


def _attn_kernel(q_ref, k_ref, v_ref, lam_ref, o_ref, *, tq, n_lat, nt, lam_init):
    j = pl.program_id(2)
    lam = lam_ref[...]
    lam_full = (jnp.exp(jnp.sum(lam[0:1] * lam[1:2], axis=-1, keepdims=True))
                - jnp.exp(jnp.sum(lam[2:3] * lam[3:4], axis=-1, keepdims=True)) + lam_init)
    dv = 2 * DH_B
    lane = lax.broadcasted_iota(jnp.int32, (1, dv), 1)

    def attend(key_lo, key_hi):
        ones = jnp.ones((ATTN_KEY_BLOCK, dv), BF16)
        for r0 in range(0, tq, ATTN_QUERY_ROWS):
            q = q_ref[r0:r0 + ATTN_QUERY_ROWS, :]
            q_maps = jnp.concatenate([jnp.where(lane < DH_B, q, 0), jnp.where(lane >= DH_B, q, 0)], axis=0)
            m = acc = None
            for k0 in range(key_lo, key_hi, ATTN_KEY_BLOCK):
                s = _dot_nt(q_maps, k_ref[k0:k0 + ATTN_KEY_BLOCK, :])
                v_ext = jnp.concatenate([v_ref[k0:k0 + ATTN_KEY_BLOCK, :], ones], axis=1)
                s_max = jnp.max(s, axis=-1, keepdims=True)
                if m is None:
                    m = s_max
                    acc = _dot(jnp.exp2(s - m).astype(BF16), v_ext)
                else:
                    m_new = jnp.maximum(m, s_max)
                    acc = jnp.exp2(m - m_new) * acc + _dot(jnp.exp2(s - m_new).astype(BF16), v_ext)
                    m = m_new
            normalised = acc[:, :dv] / acc[:, dv:]
            o_ref[r0:r0 + ATTN_QUERY_ROWS, :] = (normalised[:ATTN_QUERY_ROWS]
                                                 - lam_full * normalised[ATTN_QUERY_ROWS:])

    @pl.when(j * tq < n_lat)
    def _():
        attend(0, nt)

    @pl.when(j * tq >= n_lat)
    def _():
        attend(n_lat, nt)


def _diff_attention(p, lam, n_lat, lam_init):
    b, nt, _ = p.shape
    tq = 256
    dv = 2 * DH_B
    return pl.pallas_call(
        functools.partial(_attn_kernel, tq=tq, n_lat=n_lat, nt=nt, lam_init=lam_init),
        grid=(b, H_B, nt // tq),
        in_specs=[pl.BlockSpec((None, tq, dv), lambda bi, h, j: (bi, j, h)),
                  pl.BlockSpec((None, nt, dv), lambda bi, h, j: (bi, 0, H_B + h)),
                  pl.BlockSpec((None, nt, dv), lambda bi, h, j: (bi, 0, 2 * H_B + h)),
                  pl.BlockSpec((4, DH_B), lambda bi, h, j: (0, 0))],
        out_specs=pl.BlockSpec((None, tq, dv), lambda bi, h, j: (bi, j, h)),
        out_shape=jax.ShapeDtypeStruct((b, nt, H_B * dv), F32),
        compiler_params=_params(("parallel", "parallel", "arbitrary")),
        name="diff_attn",
    )(p, p, p, lam.astype(F32))


def _mlstm_gate_kernel(gp_ref, bias_ref, o_ref, m0_ref, *, nt, n_lat):
    x = gp_ref[...] + bias_ref[...]
    lane = lax.broadcasted_iota(jnp.int32, (1, LANES), 1)
    live = lane < 4 * H_C
    is_forget = live & ((lane & H_C) != 0)
    fwd_forget = is_forget & (lane < 2 * H_C)
    val = jnp.where(is_forget, -_softplus(-x), x)
    gates = _chunk_cumsums(val, fwd_forget, is_forget & (lane >= 2 * H_C), nt)
    o_ref[...] = gates
    nch, n_lat_ch = nt // CHUNK, n_lat // CHUNK
    b_last, m_end = [], []
    for c in range(nch):
        tile = gates[c * CHUNK:(c + 1) * CHUNK, :]
        total = jnp.where(fwd_forget, tile[CHUNK - 1:CHUNK, :], tile[0:1, :])
        a_end = total - tile + pltpu.roll(tile, H_C, axis=1)
        b_last.append(total)
        m_end.append(jnp.max(a_end, axis=0, keepdims=True))
    orders = (list(range(n_lat_ch, nch)) + list(range(n_lat_ch)), list(range(nch - 1, -1, -1)))
    for d in range(2):
        m = jnp.zeros((1, LANES), F32)
        for c in orders[d]:
            m0_ref[d, c] = m
            m = jnp.maximum(b_last[c] + m, m_end[c])


def _mlstm_gates(gp, gate_bias, n_lat):
    b, nt, _ = gp.shape
    nch = nt // CHUNK
    bias = jnp.pad(gate_bias.astype(F32).reshape(1, -1), ((0, 0), (0, LANES - 4 * H_C)))
    return pl.pallas_call(
        functools.partial(_mlstm_gate_kernel, nt=nt, n_lat=n_lat),
        grid=(b,),
        in_specs=[pl.BlockSpec((None, nt, LANES), lambda bi: (bi, 0, 0)),
                  pl.BlockSpec((1, LANES), lambda bi: (0, 0))],
        out_specs=[pl.BlockSpec((None, nt, LANES), lambda bi: (bi, 0, 0)),
                   pl.BlockSpec((None, 2, nch, 1, LANES), lambda bi: (bi, 0, 0, 0, 0))],
        out_shape=[jax.ShapeDtypeStruct((b, nt, LANES), F32),
                   jax.ShapeDtypeStruct((b, 2, nch, 1, LANES), F32)],
        compiler_params=_params(("parallel",)),
        name="mlstm_gates",
    )(gp, bias)


def _mlstm_intra_kernel(q_ref, k_ref, v_ref, g_ref, m0_ref, qs_ref, num_ref, ke_ref, st_ref, so_ref, *, rows):
    h = pl.program_id(1)
    for pair in range(rows // (2 * CHUNK)):
        p0 = pair * 2 * CHUNK
        gates = g_ref[p0:p0 + 2 * CHUNK, :]
        gates_t = gates.T
        for sub in range(2):
            r0 = p0 + sub * CHUNK
            q = q_ref[r0:r0 + CHUNK, :]
            k = k_ref[r0:r0 + CHUNK, :] * (DQK_C ** -0.5)
            v = v_ref[r0:r0 + CHUNK, :]
            qk = _dot_nt(q, k)
            tile = gates[sub * CHUNK:(sub + 1) * CHUNK, :]
            for d in range(2):
                tri, _ = _tri_masks(d)
                li_idx = 2 * d * H_C + h
                bc_idx = (2 * d + 1) * H_C + h
                bcol = _pick_col(tile, bc_idx)
                licol = _pick_col(tile, li_idx)
                brow = _pick_row(gates_t, bc_idx)[:, sub * CHUNK:(sub + 1) * CHUNK]
                lirow = _pick_row(gates_t, li_idx)[:, sub * CHUNK:(sub + 1) * CHUNK]
                m0 = _pick_col(m0_ref[d, r0 // CHUNK], bc_idx)
                dmat = jnp.where(tri, bcol - brow + lirow, -jnp.inf)
                m_intra = jnp.max(dmat, axis=-1, keepdims=True)
                m_t = jnp.maximum(bcol + m0, m_intra)
                wts = jnp.exp(dmat - m_t) * qk
                last = CHUNK - 1 if d == 0 else 0
                b_last = bcol[last:last + 1, :]
                m_new = jnp.maximum(b_last + m0, m_intra[last:last + 1, :])
                qs_ref[d, r0:r0 + CHUNK, :] = q * jnp.exp(bcol + m0 - m_t)
                num_ref[d, r0:r0 + CHUNK, :] = _dot(wts, v)
                ke_ref[d, r0:r0 + CHUNK, :] = k * jnp.exp(b_last - bcol + licol - m_new)
                den = jnp.sum(wts, axis=-1, keepdims=True)
                st_ref[d, r0:r0 + CHUNK, 0:LANES] = jnp.broadcast_to(den, (CHUNK, LANES))
                st_ref[d, r0:r0 + CHUNK, LANES:2 * LANES] = jnp.broadcast_to(jnp.exp(-m_t), (CHUNK, LANES))
                so_ref[d, r0 // CHUNK] = jnp.broadcast_to(jnp.exp(b_last + m0 - m_new), (1, LANES))


def _mlstm_intra(p, gates, m0):
    b, nt, _ = p.shape
    rows = 256
    qk_w = H_C * DQK_C
    nch = nt // CHUNK
    q_spec = pl.BlockSpec((None, rows, DQK_C), lambda bi, h, j: (bi, j, h))
    k_spec = pl.BlockSpec((None, rows, DQK_C), lambda bi, h, j: (bi, j, H_C + h))
    v_spec = pl.BlockSpec((None, rows, DV_C), lambda bi, h, j: (bi, j, 2 * qk_w // DV_C + h))
    per_chunk = lambda: pl.BlockSpec((None, None, 2, rows // CHUNK, 1, LANES), lambda bi, h, j: (bi, h, 0, j, 0, 0))
    out = lambda w: (jax.ShapeDtypeStruct((b, H_C, 2, nt, w), F32),
                     pl.BlockSpec((None, None, 2, rows, w), lambda bi, h, j: (bi, h, 0, j, 0)))
    (qs_s, qs_b), (num_s, num_b), (ke_s, ke_b), (st_s, st_b) = out(DQK_C), out(DV_C), out(DQK_C), out(2 * LANES)
    return pl.pallas_call(
        functools.partial(_mlstm_intra_kernel, rows=rows),
        grid=(b, H_C, nt // rows),
        in_specs=[q_spec, k_spec, v_spec, pl.BlockSpec((None, rows, LANES), lambda bi, h, j: (bi, j, 0)),
                  pl.BlockSpec((None, 2, rows // CHUNK, 1, LANES), lambda bi, h, j: (bi, 0, j, 0, 0))],
        out_specs=[qs_b, num_b, ke_b, st_b, per_chunk()],
        out_shape=[qs_s, num_s, ke_s, st_s, jax.ShapeDtypeStruct((b, H_C, 2, nch, 1, LANES), F32)],
        compiler_params=_params(("parallel", "parallel", "parallel")),
        name="mlstm_intra",
    )(p, p, p, gates, m0)


def _mlstm_scan_kernel(*refs, n_lat_seg, n_ctx_seg):
    dir_refs = (refs[0:6], refs[6:12])
    o_ref, c_ref = refs[12:]
    j = pl.program_id(1)
    fwd_seg, bwd_seg = _segment_orders(n_lat_seg, n_ctx_seg)
    seg_row0 = (pl.multiple_of(fwd_seg(j) * SCAN_SEG, SCAN_SEG), pl.multiple_of(bwd_seg(j) * SCAN_SEG, SCAN_SEG))

    @pl.when(j == 0)
    def _():
        c_ref[...] = jnp.zeros_like(c_ref)
        o_ref[...] = jnp.zeros_like(o_ref)

    ones = jnp.ones((CHUNK, LANES), F32)
    nsub = SCAN_SEG // CHUNK
    for sub in range(nsub):
        chains = []
        for d in range(2):
            c = sub if d == 0 else nsub - 1 - sub
            rows = slice(c * CHUNK, (c + 1) * CHUNK)
            qs_ref = dir_refs[d][1]
            for h in range(H_C):
                c_st = c_ref[d, h]
                chains.append((d, h, c, rows, c_st, _dot(qs_ref[h, rows, :], c_st)))
        for d, h, c, rows, c_st, q_c in chains:
            v_ref, _, num_ref, ke_ref, st_ref, so_ref = dir_refs[d]
            num = q_c[:, :DV_C] + num_ref[h, rows, :]
            den = q_c[:, DV_C:] + st_ref[h, rows, 0:LANES]
            inv = 1.0 / jnp.maximum(jnp.abs(den), st_ref[h, rows, LANES:2 * LANES])
            out_rows = pl.ds(seg_row0[d] + c * CHUNK, CHUNK)
            o_ref[out_rows, h * DV_C:(h + 1) * DV_C] += num * jnp.concatenate([inv] * (DV_C // LANES), axis=1)
            decay = jnp.concatenate([so_ref[h, c]] * (DV_C // LANES + 1), axis=1)
            v_ones = jnp.concatenate([v_ref[rows, h * DV_C:(h + 1) * DV_C], ones], axis=1)
            c_ref[d, h] = c_st * decay + _dot_tn(ke_ref[h, rows, :], v_ones)


def _mlstm_scan(p, qs, num, ke, st, so, n_lat):
    b, nt, _ = p.shape
    qk_w, val_w = H_C * DQK_C, H_C * DV_C
    nsub = SCAN_SEG // CHUNK
    n_lat_seg, n_ctx_seg = n_lat // SCAN_SEG, (nt - n_lat) // SCAN_SEG
    orders = _segment_orders(n_lat_seg, n_ctx_seg)
    in_specs = []
    for d in range(2):
        seg = orders[d]
        per_head = lambda w, d=d, seg=seg: pl.BlockSpec((None, H_C, None, SCAN_SEG, w),
                                                        lambda bi, j: (bi, 0, d, seg(j), 0))
        in_specs += [pl.BlockSpec((None, SCAN_SEG, val_w), lambda bi, j, seg=seg: (bi, seg(j), 2 * qk_w // val_w)),
                     per_head(DQK_C), per_head(DV_C), per_head(DQK_C), per_head(2 * LANES),
                     pl.BlockSpec((None, H_C, None, nsub, 1, LANES), lambda bi, j, d=d, seg=seg: (bi, 0, d, seg(j), 0, 0))]
    return pl.pallas_call(
        functools.partial(_mlstm_scan_kernel, n_lat_seg=n_lat_seg, n_ctx_seg=n_ctx_seg),
        grid=(b, nt // SCAN_SEG),
        in_specs=in_specs,
        out_specs=pl.BlockSpec((None, nt, val_w), lambda bi, j: (bi, 0, 0)),
        out_shape=jax.ShapeDtypeStruct((b, nt, val_w), F32),
        scratch_shapes=[pltpu.VMEM((2, H_C, DQK_C, DV_C + LANES), F32)],
        compiler_params=_params(("parallel", "arbitrary")),
        name="mlstm_scan",
    )(p, qs, num, ke, st, so, p, qs, num, ke, st, so)


def _mlstm(xs, modtab, layer, n_lat, w_in, gate_bias):
    main = 2 * H_C * DQK_C + 2 * H_C * DV_C
    w_main = w_in[:, :main].astype(BF16)
    w_gate = jnp.pad(w_in[:, main:], ((0, 0), (0, LANES - 4 * H_C))).astype(BF16)
    p = _inproj(xs, modtab, layer, w_main, n_lat)
    gp = _inproj(xs, modtab, layer, w_gate, n_lat)
    gates, m0 = _mlstm_gates(gp, gate_bias, n_lat)
    qs, num, ke, st, so = _mlstm_intra(p, gates, m0)
    return _mlstm_scan(p, qs, num, ke, st, so, n_lat), p


def kernel(x, c, ctx, c_ctx, w_mod, b_mod, ln_g, ln_b, w_in_a, conv_a, a_log_a, dt_bias_a, norm_a, w_out_a,
           w_in_b, lam_b, subln_b, w_out_b, w_in_c, gate_bias_c, norm_c, w_out_c, w1, w2):
    b, n_lat, d = x.shape
    nt = n_lat + ctx.shape[1]
    depth = w_mod.shape[0]
    xs = jnp.concatenate([x, ctx], axis=1)
    cond_rows = -(-(b + 1) // 8) * 8
    cc = jnp.concatenate([c, c_ctx[None, :], jnp.zeros((cond_rows - b - 1, d), F32)], axis=0)
    modtab = _mod_tables(cc, w_mod, b_mod)
    rope = _rope_tables(n_lat, nt)

    for i in range(depth):
        kind, j = i % N_MIXERS, i // N_MIXERS
        rows = nt if i < depth - 1 else n_lat
        if kind == 0:
            o, p = _gated_deltanet(xs, modtab, i, n_lat, w_in_a[j], conv_a[j], a_log_a[j], dt_bias_a[j])
            xs_new = _post(xs, modtab, i, o, p, 3, jnp.tile(norm_a[j], H_A), w_out_a[j].astype(BF16),
                           ln_g[i, 0], ln_b[i, 0], n_lat, rows, DK_A, "silu")
        elif kind == 1:
            lam_init = 0.8 - 0.6 * math.exp(-0.3 * i)
            p = _inproj(xs, modtab, i, w_in_b[j].astype(BF16), n_lat, rope=rope, rope_blocks=2, out_dtype=BF16)
            o = _diff_attention(p, lam_b[j], n_lat, lam_init)
            xs_new = _post(xs, modtab, i, o, None, 0, jnp.tile(subln_b[j], H_B), w_out_b[j].astype(BF16),
                           ln_g[i, 0], ln_b[i, 0], n_lat, rows, 2 * DH_B, None, out_scale=1.0 - lam_init)
        else:
            o, p = _mlstm(xs, modtab, i, n_lat, w_in_c[j], gate_bias_c[j])
            xs_new = _post(xs, modtab, i, o, p, 2, norm_c[j], w_out_c[j].astype(BF16),
                           ln_g[i, 0], ln_b[i, 0], n_lat, rows, DV_C, "sigmoid")
        xs = _mlp(xs_new, modtab, i, w1[i].astype(BF16), w2[i].astype(BF16), ln_g[i, 1], ln_b[i, 1], n_lat)
    return xs
```

```python
import functools
import math

import jax
import jax.numpy as jnp
from jax import lax
from jax.experimental import pallas as pl
from jax.experimental.pallas import tpu as pltpu

F32 = jnp.float32
BF16 = jnp.bfloat16
HIGHEST = lax.Precision.HIGHEST

DEPTH = 4
N_MIXERS = 3
ALPHA = (2.0 * DEPTH) ** 0.25
EPS = 1e-6
LN_EPS = 1e-5
GRID_W = 64
ROPE_BASE = 10000.0
LOG2_E = math.log2(math.e)
CHUNK = 64
SCAN_SEG = 256
ATTN_QUERY_ROWS = 128
ATTN_KEY_BLOCK = 256
LANES = 128
SUBLANES = 8

H_A, DK_A = 8, 128
CONV_K = 5
H_B, DH_B = 8, 64
H_C, DQK_C, DV_C = 4, 128, 256

VMEM_LIMIT = 56 * 1024 * 1024


def _params(sem, vmem=VMEM_LIMIT):
    return pltpu.CompilerParams(dimension_semantics=sem, vmem_limit_bytes=vmem)


def _sigmoid(x):
    return 1.0 / (1.0 + jnp.exp(-x))


def _softplus(x):
    return jnp.maximum(x, 0.0) + jnp.log1p(jnp.exp(-jnp.abs(x)))


def _dot(a, b):
    return jnp.dot(a, b, preferred_element_type=F32)


def _dot_nt(a, b):
    return lax.dot_general(a, b, (((1,), (1,)), ((), ())), preferred_element_type=F32)


def _dot_tn(a, b):
    return lax.dot_general(a, b, (((0,), (0,)), ((), ())), preferred_element_type=F32)


def _row_tile(nt):
    for tm in (768, 512, 256):
        if nt % tm == 0:
            return tm
    raise ValueError(f"sequence length {nt} must be a multiple of 256")


def _mod_kernel(c_ref, w_ref, b_ref, o_ref):
    cc = c_ref[...]
    s = cc * _sigmoid(cc)
    o_ref[...] = jnp.dot(s, w_ref[...], precision=HIGHEST, preferred_element_type=F32) + b_ref[...]


def _mod_tables(cc, w_mod, b_mod):
    r, d = cc.shape
    depth, _, f = w_mod.shape
    tn = 1536
    out = pl.pallas_call(
        _mod_kernel,
        grid=(depth, f // tn),
        in_specs=[pl.BlockSpec((r, d), lambda l, n: (0, 0)),
                  pl.BlockSpec((None, d, tn), lambda l, n: (l, 0, n)),
                  pl.BlockSpec((None, 1, tn), lambda l, n: (l, 0, n))],
        out_specs=pl.BlockSpec((None, r, tn), lambda l, n: (l, 0, n)),
        out_shape=jax.ShapeDtypeStruct((depth, r, f), F32),
        compiler_params=_params(("parallel", "parallel")),
        name="mod_tables",
    )(cc, w_mod, b_mod.reshape(depth, 1, f))
    return out.reshape(depth, r, 6, d)


def _mod_specs(layer, ctx_row):
    def lat(b, *_):
        return (layer, b, 0, 0)

    def ctx(b, *_):
        return (layer, ctx_row, 0, 0)
    return lat, ctx


def _select_mod(ml_ref, mc_ref, idx, is_ctx):
    return jnp.where(is_ctx, mc_ref[idx:idx + 1, :], ml_ref[idx:idx + 1, :])


def _inproj_kernel(x_ref, ml_ref, mc_ref, w_ref, *rest, tm, n_lat, rope_blocks):
    if rope_blocks:
        cos_ref, sin_ref, o_ref, h_ref = rest
    else:
        o_ref, h_ref = rest
    j = pl.program_id(1)
    n = pl.program_id(2)

    @pl.when(n == 0)
    def _():
        row = j * tm + lax.broadcasted_iota(jnp.int32, (tm, 1), 0)
        is_ctx = row >= n_lat
        shift = _select_mod(ml_ref, mc_ref, 0, is_ctx)
        scale = _select_mod(ml_ref, mc_ref, 1, is_ctx)
        h_ref[...] = (x_ref[...] * (1.0 + scale) + shift).astype(h_ref.dtype)

    acc = _dot(h_ref[...], w_ref[...])
    if not rope_blocks:
        o_ref[...] = acc.astype(o_ref.dtype)
        return

    @pl.when(n >= rope_blocks)
    def _():
        o_ref[...] = acc.astype(o_ref.dtype)

    @pl.when(n < rope_blocks)
    def _():
        q_scale = jnp.where(n == 0, (DH_B ** -0.5) * LOG2_E, 1.0)
        cos = cos_ref[...] * q_scale
        sin = sin_ref[...] * q_scale
        lane = lax.broadcasted_iota(jnp.int32, (1, LANES), 1)
        first_half = (lane & 31) < 16
        for i in range(acc.shape[1] // LANES):
            xb = acc[:, i * LANES:(i + 1) * LANES]
            partner = jnp.where(first_half, pltpu.roll(xb, LANES - 16, axis=1), pltpu.roll(xb, 16, axis=1))
            o_ref[:, i * LANES:(i + 1) * LANES] = (xb * cos + partner * sin).astype(o_ref.dtype)


def _inproj(xs, modtab, layer, w, n_lat, rope=None, rope_blocks=0, out_dtype=F32):
    b, nt, d = xs.shape
    f = w.shape[1]
    tm = _row_tile(nt)
    tn = min(f, 1024)
    lat, ctx = _mod_specs(layer, b)
    in_specs = [pl.BlockSpec((None, tm, d), lambda bi, j, n: (bi, j, 0)),
                pl.BlockSpec((None, None, 6, d), lat),
                pl.BlockSpec((None, None, 6, d), ctx),
                pl.BlockSpec((d, tn), lambda bi, j, n: (0, n))]
    args = [xs, modtab, modtab, w]
    if rope_blocks:
        in_specs += [pl.BlockSpec((tm, LANES), lambda bi, j, n: (j, 0))] * 2
        args += list(rope)
    return pl.pallas_call(
        functools.partial(_inproj_kernel, tm=tm, n_lat=n_lat, rope_blocks=rope_blocks),
        grid=(b, nt // tm, f // tn),
        in_specs=in_specs,
        out_specs=pl.BlockSpec((None, tm, tn), lambda bi, j, n: (bi, j, n)),
        out_shape=jax.ShapeDtypeStruct((b, nt, f), out_dtype),
        scratch_shapes=[pltpu.VMEM((tm, d), BF16)],
        compiler_params=_params(("parallel", "parallel", "arbitrary")),
        name="inproj",
    )(*args)


def _layer_norm(v, g, b):
    mu = jnp.mean(v, axis=-1, keepdims=True)
    vc = v - mu
    var = jnp.mean(vc * vc, axis=-1, keepdims=True)
    return vc * lax.rsqrt(var + LN_EPS) * g + b


def _post_kernel(x_ref, ml_ref, mc_ref, o_ref, *rest, tm, n_lat, group, gate, out_scale):
    if gate is None:
        gn_ref, w_ref, lg_ref, lb_ref, out_ref = rest
    else:
        z_ref, gn_ref, w_ref, lg_ref, lb_ref, out_ref = rest
    j = pl.program_id(1)
    row = j * tm + lax.broadcasted_iota(jnp.int32, (tm, 1), 0)
    is_ctx = row >= n_lat
    width = o_ref.shape[-1]
    parts = []
    for g0 in range(0, width, group):
        og = o_ref[:, g0:g0 + group]
        ms = jnp.mean(og * og, axis=-1, keepdims=True)
        yg = og * lax.rsqrt(ms + EPS) * gn_ref[:, g0:g0 + group]
        if gate == "silu":
            z = z_ref[:, g0:g0 + group]
            yg = yg * (z * _sigmoid(z))
        elif gate == "sigmoid":
            yg = yg * _sigmoid(z_ref[:, g0:g0 + group])
        else:
            yg = yg * out_scale
        parts.append(yg.astype(BF16))
    y = _dot(jnp.concatenate(parts, axis=-1), w_ref[...])
    res_gate = _select_mod(ml_ref, mc_ref, 2, is_ctx)
    out_ref[...] = _layer_norm(ALPHA * x_ref[...] + res_gate * y, lg_ref[...], lb_ref[...])


def _post(xs, modtab, layer, o, z_arr, z_block, gnorm, w_out, ln_g, ln_b, n_lat, rows, group, gate, out_scale=1.0):
    b, _, d = xs.shape
    width = o.shape[-1]
    tm = _row_tile(rows)
    lat, ctx = _mod_specs(layer, b)
    in_specs = [pl.BlockSpec((None, tm, d), lambda bi, j: (bi, j, 0)),
                pl.BlockSpec((None, None, 6, d), lat),
                pl.BlockSpec((None, None, 6, d), ctx),
                pl.BlockSpec((None, tm, width), lambda bi, j: (bi, j, 0))]
    args = [xs, modtab, modtab, o]
    if gate is not None:
        in_specs.append(pl.BlockSpec((None, tm, width), lambda bi, j: (bi, j, z_block)))
        args.append(z_arr)
    in_specs += [pl.BlockSpec((1, width), lambda bi, j: (0, 0)),
                 pl.BlockSpec((width, d), lambda bi, j: (0, 0)),
                 pl.BlockSpec((1, d), lambda bi, j: (0, 0)),
                 pl.BlockSpec((1, d), lambda bi, j: (0, 0))]
    args += [gnorm.reshape(1, width), w_out, ln_g.reshape(1, d), ln_b.reshape(1, d)]
    return pl.pallas_call(
        functools.partial(_post_kernel, tm=tm, n_lat=n_lat, group=group, gate=gate, out_scale=out_scale),
        grid=(b, rows // tm),
        in_specs=in_specs,
        out_specs=pl.BlockSpec((None, tm, d), lambda bi, j: (bi, j, 0)),
        out_shape=jax.ShapeDtypeStruct((b, rows, d), F32),
        compiler_params=_params(("parallel", "parallel")),
        name="mixer_out",
    )(*args)


def _mlp_kernel(x_ref, ml_ref, mc_ref, w1_ref, w2_ref, lg_ref, lb_ref, out_ref, h_ref, acc_ref, *, tm, n_lat):
    j = pl.program_id(1)
    k = pl.program_id(2)
    row = j * tm + lax.broadcasted_iota(jnp.int32, (tm, 1), 0)
    is_ctx = row >= n_lat

    @pl.when(k == 0)
    def _():
        shift = _select_mod(ml_ref, mc_ref, 3, is_ctx)
        scale = _select_mod(ml_ref, mc_ref, 4, is_ctx)
        h_ref[...] = (x_ref[...] * (1.0 + scale) + shift).astype(h_ref.dtype)
        acc_ref[...] = jnp.zeros_like(acc_ref)

    a = jnp.maximum(_dot(h_ref[...], w1_ref[...]), 0.0)
    acc_ref[...] += _dot((a * a).astype(BF16), w2_ref[...])

    @pl.when(k == pl.num_programs(2) - 1)
    def _():
        res_gate = _select_mod(ml_ref, mc_ref, 5, is_ctx)
        out_ref[...] = _layer_norm(ALPHA * x_ref[...] + res_gate * acc_ref[...], lg_ref[...], lb_ref[...])


def _mlp(xs, modtab, layer, w1, w2, ln_g, ln_b, n_lat):
    b, rows, d = xs.shape
    dff = w1.shape[1]
    tm = _row_tile(rows)
    tf = 512
    lat, ctx = _mod_specs(layer, b)
    return pl.pallas_call(
        functools.partial(_mlp_kernel, tm=tm, n_lat=n_lat),
        grid=(b, rows // tm, dff // tf),
        in_specs=[pl.BlockSpec((None, tm, d), lambda bi, j, k: (bi, j, 0)),
                  pl.BlockSpec((None, None, 6, d), lat),
                  pl.BlockSpec((None, None, 6, d), ctx),
                  pl.BlockSpec((d, tf), lambda bi, j, k: (0, k)),
                  pl.BlockSpec((tf, d), lambda bi, j, k: (k, 0)),
                  pl.BlockSpec((1, d), lambda bi, j, k: (0, 0)),
                  pl.BlockSpec((1, d), lambda bi, j, k: (0, 0))],
        out_specs=pl.BlockSpec((None, tm, d), lambda bi, j, k: (bi, j, 0)),
        out_shape=jax.ShapeDtypeStruct((b, rows, d), F32),
        scratch_shapes=[pltpu.VMEM((tm, d), BF16), pltpu.VMEM((tm, d), F32)],
        compiler_params=_params(("parallel", "parallel", "arbitrary")),
        name="mlp",
    )(xs, modtab, modtab, w1, w2, ln_g.reshape(1, d), ln_b.reshape(1, d))


def _segment_orders(n_lat_seg, n_ctx_seg):
    def fwd(j):
        return jnp.where(j < n_ctx_seg, j + n_lat_seg, j - n_ctx_seg)

    def bwd(j):
        return n_lat_seg + n_ctx_seg - 1 - j
    return fwd, bwd


def _cumsum_mats():
    r = lax.broadcasted_iota(jnp.int32, (CHUNK, CHUNK), 0)
    c = lax.broadcasted_iota(jnp.int32, (CHUNK, CHUNK), 1)
    return (c <= r).astype(F32), (c >= r).astype(F32)


def _chunk_cumsums(val, fwd_cols, bwd_cols, rows):
    lower, upper = _cumsum_mats()
    out = []
    for r0 in range(0, rows, CHUNK):
        v = val[r0:r0 + CHUNK]
        cf = jnp.dot(lower, v, precision=HIGHEST, preferred_element_type=F32)
        cb = jnp.dot(upper, v, precision=HIGHEST, preferred_element_type=F32)
        out.append(jnp.where(fwd_cols, cf, jnp.where(bwd_cols, cb, v)))
    return jnp.concatenate(out, axis=0)


def _pick_col(tile, idx):
    lane = lax.broadcasted_iota(jnp.int32, tile.shape, 1)
    return jnp.sum(jnp.where(lane == idx, tile, 0.0), axis=1, keepdims=True)


def _pick_row(tile, idx):
    sub = lax.broadcasted_iota(jnp.int32, tile.shape, 0)
    return jnp.sum(jnp.where(sub == idx, tile, 0.0), axis=0, keepdims=True)


def _tri_masks(direction):
    r = lax.broadcasted_iota(jnp.int32, (CHUNK, CHUNK), 0)
    c = lax.broadcasted_iota(jnp.int32, (CHUNK, CHUNK), 1)
    if direction == 0:
        return c <= r, c < r
    return c >= r, c > r


def _unit_tri_inverses(mats):
    r = lax.broadcasted_iota(jnp.int32, (CHUNK, CHUNK), 0)
    c = lax.broadcasted_iota(jnp.int32, (CHUNK, CHUNK), 1)
    differ = r ^ c
    eye = jnp.where(r == c, 1.0, 0.0)
    ts = [eye - jnp.where(differ < 2, a, 0.0) for a in mats]
    size = 2
    while size < CHUNK:
        joins = (differ >= size) & (differ < 2 * size)
        ets = [_dot(jnp.where(joins, a, 0.0), t) for a, t in zip(mats, ts)]
        ts = [t - _dot(t, et) for t, et in zip(ts, ets)]
        size *= 2
    return ts


def _gdn_gate_kernel(gp_ref, alog_ref, dtb_ref, o_ref, *, rows):
    x = gp_ref[...]
    lane = lax.broadcasted_iota(jnp.int32, (1, LANES), 1)
    live = lane < 4 * H_A
    is_decay = live & ((lane & (2 * H_A - 1)) < H_A)
    g = -jnp.exp(alog_ref[...]) * _softplus(x + dtb_ref[...])
    val = jnp.where(is_decay, g, _sigmoid(x))
    o_ref[...] = _chunk_cumsums(val, is_decay & (lane < 2 * H_A), is_decay & (lane >= 2 * H_A), rows)


def _gdn_gates(gp, a_log, dt_bias):
    b, nt, _ = gp.shape
    rows = 256
    pad = lambda t: jnp.pad(jnp.stack([t[0], jnp.zeros_like(t[0]), t[1], jnp.zeros_like(t[1])]).reshape(1, -1),
                            ((0, 0), (0, LANES - 4 * H_A)))
    return pl.pallas_call(
        functools.partial(_gdn_gate_kernel, rows=rows),
        grid=(b, nt // rows),
        in_specs=[pl.BlockSpec((None, rows, LANES), lambda bi, j: (bi, j, 0)),
                  pl.BlockSpec((1, LANES), lambda bi, j: (0, 0)),
                  pl.BlockSpec((1, LANES), lambda bi, j: (0, 0))],
        out_specs=pl.BlockSpec((None, rows, LANES), lambda bi, j: (bi, j, 0)),
        out_shape=jax.ShapeDtypeStruct((b, nt, LANES), F32),
        compiler_params=_params(("parallel", "parallel")),
        name="gdn_gates",
    )(gp, pad(a_log.astype(F32)), pad(dt_bias.astype(F32)))


def _gdn_conv_kernel(p_ref, w_ref, o_ref, pad_ref, *, n_lat, nt):
    cb = pl.program_id(1)
    half = CONV_K // 2
    zeros = jnp.zeros((SUBLANES, DK_A), F32)
    segments = ((0, n_lat, SUBLANES), (n_lat, nt, 2 * SUBLANES))
    pad_ref[0:SUBLANES, :] = zeros
    for lo, hi, off in segments:
        pad_ref[lo + off:hi + off, :] = p_ref[lo:hi, :]
        pad_ref[hi + off:hi + off + SUBLANES, :] = zeros
    for lo, hi, off in segments:
        acc = p_ref[lo:hi, :] * w_ref[half:half + 1, :]
        for s in range(-half, half + 1):
            if s != 0:
                acc = acc + pad_ref[lo + off + s:hi + off + s, :] * w_ref[half + s:half + s + 1, :]
        y = acc * _sigmoid(acc)
        inv = lax.rsqrt(jnp.sum(y * y, axis=-1, keepdims=True) + EPS)
        factor = jnp.where(cb < H_A, inv * (DK_A ** -0.5), jnp.where(cb < 2 * H_A, inv, 1.0))
        o_ref[lo:hi, :] = y * factor


def _gdn_conv(p, conv_w, n_lat):
    b, nt, _ = p.shape
    nblk = 3 * H_A
    return pl.pallas_call(
        functools.partial(_gdn_conv_kernel, n_lat=n_lat, nt=nt),
        grid=(b, nblk),
        in_specs=[pl.BlockSpec((None, nt, DK_A), lambda bi, cb: (bi, 0, cb)),
                  pl.BlockSpec((CONV_K, DK_A), lambda bi, cb: (0, cb))],
        out_specs=pl.BlockSpec((None, nt, DK_A), lambda bi, cb: (bi, 0, cb)),
        out_shape=jax.ShapeDtypeStruct((b, nt, nblk * DK_A), F32),
        scratch_shapes=[pltpu.VMEM((nt + 3 * SUBLANES, DK_A), F32)],
        compiler_params=_params(("parallel", "parallel")),
        name="gdn_conv",
    )(p, conv_w.astype(F32))


def _gdn_intra_kernel(q_ref, k_ref, v_ref, g_ref, u_ref, w_ref, qg_ref, kd_ref, qk_ref, gl_ref, *, rows):
    h = pl.program_id(1)
    chains = []
    for pair in range(rows // (2 * CHUNK)):
        p0 = pair * 2 * CHUNK
        gates = g_ref[p0:p0 + 2 * CHUNK, :]
        gates_t = gates.T
        for sub in range(2):
            r0 = p0 + sub * CHUNK
            q = q_ref[r0:r0 + CHUNK, :]
            k = k_ref[r0:r0 + CHUNK, :]
            v = v_ref[r0:r0 + CHUNK, :]
            kk = _dot_nt(k, k)
            qk = _dot_nt(q, k)
            tile = gates[sub * CHUNK:(sub + 1) * CHUNK, :]
            for d in range(2):
                tri, strict = _tri_masks(d)
                gcol = _pick_col(tile, 2 * H_A * d + h)
                beta = _pick_col(tile, 2 * H_A * d + H_A + h)
                grow = _pick_row(gates_t, 2 * H_A * d + h)[:, sub * CHUNK:(sub + 1) * CHUNK]
                decay = jnp.where(tri, jnp.exp(jnp.where(tri, gcol - grow, 0.0)), 0.0)
                eg = jnp.exp(gcol)
                last = CHUNK - 1 if d == 0 else 0
                g_last = gcol[last:last + 1, :]
                qg_ref[d, r0:r0 + CHUNK, :] = q * eg
                kd_ref[d, r0:r0 + CHUNK, :] = k * jnp.exp(g_last - gcol)
                qk_ref[r0:r0 + CHUNK, d * CHUNK:(d + 1) * CHUNK] = jnp.where(tri, qk * decay, 0.0)
                gl_ref[d, r0 // CHUNK] = jnp.broadcast_to(jnp.exp(g_last), (1, LANES))
                chains.append((d, r0, jnp.where(strict, kk * beta * decay, 0.0),
                               jnp.concatenate([v * beta, k * (beta * eg)], axis=1)))
    t_invs = _unit_tri_inverses([a for _, _, a, _ in chains])
    for (d, r0, _, rhs), t_inv in zip(chains, t_invs):
        uw = _dot(t_inv, rhs)
        u_ref[d, r0:r0 + CHUNK, :] = uw[:, :DK_A]
        w_ref[d, r0:r0 + CHUNK, :] = uw[:, DK_A:]


def _gdn_intra(qkv, gates):
    b, nt, _ = qkv.shape
    rows = _row_tile(nt)
    nch = nt // CHUNK
    big = jax.ShapeDtypeStruct((b, H_A, 2, nt, DK_A), F32)
    big_spec = pl.BlockSpec((None, None, 2, rows, DK_A), lambda bi, h, j: (bi, h, 0, j, 0))
    col = lambda off: pl.BlockSpec((None, rows, DK_A), lambda bi, h, j: (bi, j, off + h))
    return pl.pallas_call(
        functools.partial(_gdn_intra_kernel, rows=rows),
        grid=(b, H_A, nt // rows),
        in_specs=[col(0), col(H_A), col(2 * H_A),
                  pl.BlockSpec((None, rows, LANES), lambda bi, h, j: (bi, j, 0))],
        out_specs=[big_spec, big_spec, big_spec, big_spec,
                   pl.BlockSpec((None, None, rows, 2 * CHUNK), lambda bi, h, j: (bi, h, j, 0)),
                   pl.BlockSpec((None, None, 2, rows // CHUNK, 1, LANES), lambda bi, h, j: (bi, h, 0, j, 0, 0))],
        out_shape=[big, big, big, big,
                   jax.ShapeDtypeStruct((b, H_A, nt, 2 * CHUNK), F32),
                   jax.ShapeDtypeStruct((b, H_A, 2, nch, 1, LANES), F32)],
        compiler_params=_params(("parallel", "parallel", "parallel")),
        name="gdn_intra",
    )(qkv, qkv, qkv, gates)


def _gdn_scan_kernel(*refs, n_lat_seg, n_ctx_seg):
    dir_refs = (refs[0:6], refs[6:12])
    o_ref, s_ref = refs[12:]
    j = pl.program_id(1)
    fwd_seg, bwd_seg = _segment_orders(n_lat_seg, n_ctx_seg)
    seg_row0 = (pl.multiple_of(fwd_seg(j) * SCAN_SEG, SCAN_SEG), pl.multiple_of(bwd_seg(j) * SCAN_SEG, SCAN_SEG))

    @pl.when(j == 0)
    def _():
        s_ref[...] = jnp.zeros_like(s_ref)
        o_ref[...] = jnp.zeros_like(o_ref)

    nsub = SCAN_SEG // CHUNK
    for sub in range(nsub):
        chains = []
        for d in range(2):
            c = sub if d == 0 else nsub - 1 - sub
            rows = slice(c * CHUNK, (c + 1) * CHUNK)
            _, w_ref, qg_ref, _, _, _ = dir_refs[d]
            for h in range(H_A):
                state = s_ref[d, h]
                wq = _dot(jnp.concatenate([w_ref[h, rows, :], qg_ref[h, rows, :]], axis=0), state)
                chains.append((d, h, c, rows, state, wq))
        for d, h, c, rows, state, wq in chains:
            u_ref, _, _, kd_ref, qk_ref, gl_ref = dir_refs[d]
            v_new = u_ref[h, rows, :] - wq[:CHUNK]
            qk = qk_ref[h, rows, d * CHUNK:(d + 1) * CHUNK]
            out_rows = pl.ds(seg_row0[d] + c * CHUNK, CHUNK)
            o_ref[out_rows, h * DK_A:(h + 1) * DK_A] += wq[CHUNK:] + _dot(qk, v_new)
            s_ref[d, h] = state * gl_ref[h, c] + _dot_tn(kd_ref[h, rows, :], v_new)


def _gdn_scan(u, w, qg, kd, qk, gl, n_lat):
    b, h, _, nt, dk = u.shape
    nsub = SCAN_SEG // CHUNK
    n_lat_seg, n_ctx_seg = n_lat // SCAN_SEG, (nt - n_lat) // SCAN_SEG
    orders = _segment_orders(n_lat_seg, n_ctx_seg)
    in_specs = []
    for d in range(2):
        seg = orders[d]
        big = pl.BlockSpec((None, h, None, SCAN_SEG, dk), lambda bi, j, d=d, seg=seg: (bi, 0, d, seg(j), 0))
        in_specs += [big, big, big, big,
                     pl.BlockSpec((None, h, SCAN_SEG, 2 * CHUNK), lambda bi, j, seg=seg: (bi, 0, seg(j), 0)),
                     pl.BlockSpec((None, h, None, nsub, 1, LANES), lambda bi, j, d=d, seg=seg: (bi, 0, d, seg(j), 0, 0))]
    return pl.pallas_call(
        functools.partial(_gdn_scan_kernel, n_lat_seg=n_lat_seg, n_ctx_seg=n_ctx_seg),
        grid=(b, nt // SCAN_SEG),
        in_specs=in_specs,
        out_specs=pl.BlockSpec((None, nt, h * dk), lambda bi, j: (bi, 0, 0)),
        out_shape=jax.ShapeDtypeStruct((b, nt, h * dk), F32),
        scratch_shapes=[pltpu.VMEM((2, h, dk, dk), F32)],
        compiler_params=_params(("parallel", "arbitrary")),
        name="gdn_scan",
    )(u, w, qg, kd, qk, gl, u, w, qg, kd, qk, gl)


def _gated_deltanet(xs, modtab, layer, n_lat, w_in, conv_w, a_log, dt_bias):
    key = H_A * DK_A
    w_main = w_in[:, :4 * key].astype(BF16)
    w_gate = jnp.pad(w_in[:, 4 * key:], ((0, 0), (0, LANES - 4 * H_A))).astype(BF16)
    p = _inproj(xs, modtab, layer, w_main, n_lat)
    gp = _inproj(xs, modtab, layer, w_gate, n_lat)
    gates = _gdn_gates(gp, a_log, dt_bias)
    qkv = _gdn_conv(p, conv_w, n_lat)
    u, w, qg, kd, qk, gl = _gdn_intra(qkv, gates)
    o = _gdn_scan(u, w, qg, kd, qk, gl, n_lat)
    return o, p


def _rope_tables(n_lat, nt):
    rows = n_lat // GRID_W
    r = jnp.repeat(jnp.arange(rows, dtype=F32), GRID_W)
    col = jnp.tile(jnp.arange(GRID_W, dtype=F32), rows)
    half = DH_B // 2
    inv = ROPE_BASE ** (-jnp.arange(0, half, 2, dtype=F32) / half)
    ang_r, ang_c = r[:, None] * inv, col[:, None] * inv
    cos = jnp.concatenate([jnp.cos(ang_r)] * 2 + [jnp.cos(ang_c)] * 2, axis=-1)
    sin = jnp.concatenate([-jnp.sin(ang_r), jnp.sin(ang_r), -jnp.sin(ang_c), jnp.sin(ang_c)], axis=-1)
    cos = jnp.concatenate([jnp.tile(cos, (1, LANES // DH_B)), jnp.ones((nt - n_lat, LANES), F32)], axis=0)
    sin = jnp.concatenate([jnp.tile(sin, (1, LANES // DH_B)), jnp.zeros((nt - n_lat, LANES), F32)], axis=0)
    return cos, sin


def _attn_kernel(q_ref, k_ref, v_ref, lam_ref, o_ref, *, tq, n_lat, nt, lam_init):
    j = pl.program_id(2)
    lam = lam_ref[...]
    lam_full = (jnp.exp(jnp.sum(lam[0:1] * lam[1:2], axis=-1, keepdims=True))
                - jnp.exp(jnp.sum(lam[2:3] * lam[3:4], axis=-1, keepdims=True)) + lam_init)
    dv = 2 * DH_B
    lane = lax.broadcasted_iota(jnp.int32, (1, dv), 1)

    def attend(key_lo, key_hi):
        block = min(ATTN_KEY_BLOCK, key_hi - key_lo)
        ones = jnp.ones((block, dv), BF16)
        for r0 in range(0, tq, ATTN_QUERY_ROWS):
            q = q_ref[r0:r0 + ATTN_QUERY_ROWS, :]
            q_maps = jnp.concatenate([jnp.where(lane < DH_B, q, 0), jnp.where(lane >= DH_B, q, 0)], axis=0)
            m = acc = None
            for k0 in range(key_lo, key_hi, block):
                s = _dot_nt(q_maps, k_ref[k0:k0 + block, :])
                v_ext = jnp.concatenate([v_ref[k0:k0 + block, :], ones], axis=1)
                s_max = jnp.max(s, axis=-1, keepdims=True)
                if m is None:
                    m = s_max
                    acc = _dot(jnp.exp2(s - m).astype(BF16), v_ext)
                else:
                    m_new = jnp.maximum(m, s_max)
                    acc = jnp.exp2(m - m_new) * acc + _dot(jnp.exp2(s - m_new).astype(BF16), v_ext)
                    m = m_new
            normalised = acc[:, :dv] / acc[:, dv:]
            o_ref[r0:r0 + ATTN_QUERY_ROWS, :] = (normalised[:ATTN_QUERY_ROWS]
                                                 - lam_full * normalised[ATTN_QUERY_ROWS:])

    @pl.when(j * tq < n_lat)
    def _():
        attend(0, nt)

    @pl.when(j * tq >= n_lat)
    def _():
        attend(n_lat, nt)


def _diff_attention(p, lam, n_lat, lam_init):
    b, nt, _ = p.shape
    tq = 256
    dv = 2 * DH_B
    return pl.pallas_call(
        functools.partial(_attn_kernel, tq=tq, n_lat=n_lat, nt=nt, lam_init=lam_init),
        grid=(b, H_B, nt // tq),
        in_specs=[pl.BlockSpec((None, tq, dv), lambda bi, h, j: (bi, j, h)),
                  pl.BlockSpec((None, nt, dv), lambda bi, h, j: (bi, 0, H_B + h)),
                  pl.BlockSpec((None, nt, dv), lambda bi, h, j: (bi, 0, 2 * H_B + h)),
                  pl.BlockSpec((4, DH_B), lambda bi, h, j: (0, 0))],
        out_specs=pl.BlockSpec((None, tq, dv), lambda bi, h, j: (bi, j, h)),
        out_shape=jax.ShapeDtypeStruct((b, nt, H_B * dv), F32),
        compiler_params=_params(("parallel", "parallel", "arbitrary")),
        name="diff_attn",
    )(p, p, p, lam.astype(F32))


def _mlstm_gate_kernel(gp_ref, bias_ref, o_ref, m0_ref, *, nt, n_lat):
    x = gp_ref[...] + bias_ref[...]
    lane = lax.broadcasted_iota(jnp.int32, (1, LANES), 1)
    live = lane < 4 * H_C
    is_forget = live & ((lane & H_C) != 0)
    fwd_forget = is_forget & (lane < 2 * H_C)
    val = jnp.where(is_forget, -_softplus(-x), x)
    gates = _chunk_cumsums(val, fwd_forget, is_forget & (lane >= 2 * H_C), nt)
    o_ref[...] = gates
    nch, n_lat_ch = nt // CHUNK, n_lat // CHUNK
    b_last, m_end = [], []
    for c in range(nch):
        tile = gates[c * CHUNK:(c + 1) * CHUNK, :]
        total = jnp.where(fwd_forget, tile[CHUNK - 1:CHUNK, :], tile[0:1, :])
        a_end = total - tile + pltpu.roll(tile, H_C, axis=1)
        b_last.append(total)
        m_end.append(jnp.max(a_end, axis=0, keepdims=True))
    orders = (list(range(n_lat_ch, nch)) + list(range(n_lat_ch)), list(range(nch - 1, -1, -1)))
    for d in range(2):
        m = jnp.zeros((1, LANES), F32)
        for c in orders[d]:
            m0_ref[d, c] = m
            m = jnp.maximum(b_last[c] + m, m_end[c])


def _mlstm_gates(gp, gate_bias, n_lat):
    b, nt, _ = gp.shape
    nch = nt // CHUNK
    bias = jnp.pad(gate_bias.astype(F32).reshape(1, -1), ((0, 0), (0, LANES - 4 * H_C)))
    return pl.pallas_call(
        functools.partial(_mlstm_gate_kernel, nt=nt, n_lat=n_lat),
        grid=(b,),
        in_specs=[pl.BlockSpec((None, nt, LANES), lambda bi: (bi, 0, 0)),
                  pl.BlockSpec((1, LANES), lambda bi: (0, 0))],
        out_specs=[pl.BlockSpec((None, nt, LANES), lambda bi: (bi, 0, 0)),
                   pl.BlockSpec((None, 2, nch, 1, LANES), lambda bi: (bi, 0, 0, 0, 0))],
        out_shape=[jax.ShapeDtypeStruct((b, nt, LANES), F32),
                   jax.ShapeDtypeStruct((b, 2, nch, 1, LANES), F32)],
        compiler_params=_params(("parallel",)),
        name="mlstm_gates",
    )(gp, bias)


def _mlstm_intra_kernel(q_ref, k_ref, v_ref, g_ref, m0_ref, qs_ref, num_ref, ke_ref, st_ref, so_ref, *, rows):
    h = pl.program_id(1)
    for pair in range(rows // (2 * CHUNK)):
        p0 = pair * 2 * CHUNK
        gates = g_ref[p0:p0 + 2 * CHUNK, :]
        gates_t = gates.T
        for sub in range(2):
            r0 = p0 + sub * CHUNK
            q = q_ref[r0:r0 + CHUNK, :]
            k = k_ref[r0:r0 + CHUNK, :] * (DQK_C ** -0.5)
            v = v_ref[r0:r0 + CHUNK, :]
            qk = _dot_nt(q, k)
            tile = gates[sub * CHUNK:(sub + 1) * CHUNK, :]
            for d in range(2):
                tri, _ = _tri_masks(d)
                li_idx = 2 * d * H_C + h
                bc_idx = (2 * d + 1) * H_C + h
                bcol = _pick_col(tile, bc_idx)
                licol = _pick_col(tile, li_idx)
                brow = _pick_row(gates_t, bc_idx)[:, sub * CHUNK:(sub + 1) * CHUNK]
                lirow = _pick_row(gates_t, li_idx)[:, sub * CHUNK:(sub + 1) * CHUNK]
                m0 = _pick_col(m0_ref[d, r0 // CHUNK], bc_idx)
                dmat = jnp.where(tri, bcol - brow + lirow, -jnp.inf)
                m_intra = jnp.max(dmat, axis=-1, keepdims=True)
                m_t = jnp.maximum(bcol + m0, m_intra)
                wts = jnp.exp(dmat - m_t) * qk
                last = CHUNK - 1 if d == 0 else 0
                b_last = bcol[last:last + 1, :]
                m_new = jnp.maximum(b_last + m0, m_intra[last:last + 1, :])
                qs_ref[d, r0:r0 + CHUNK, :] = q * jnp.exp(bcol + m0 - m_t)
                num_ref[d, r0:r0 + CHUNK, :] = _dot(wts, v)
                ke_ref[d, r0:r0 + CHUNK, :] = k * jnp.exp(b_last - bcol + licol - m_new)
                den = jnp.sum(wts, axis=-1, keepdims=True)
                st_ref[d, r0:r0 + CHUNK, 0:LANES] = jnp.broadcast_to(den, (CHUNK, LANES))
                st_ref[d, r0:r0 + CHUNK, LANES:2 * LANES] = jnp.broadcast_to(jnp.exp(-m_t), (CHUNK, LANES))
                so_ref[d, r0 // CHUNK] = jnp.broadcast_to(jnp.exp(b_last + m0 - m_new), (1, LANES))


def _mlstm_intra(p, gates, m0):
    b, nt, _ = p.shape
    rows = 256
    qk_w = H_C * DQK_C
    nch = nt // CHUNK
    q_spec = pl.BlockSpec((None, rows, DQK_C), lambda bi, h, j: (bi, j, h))
    k_spec = pl.BlockSpec((None, rows, DQK_C), lambda bi, h, j: (bi, j, H_C + h))
    v_spec = pl.BlockSpec((None, rows, DV_C), lambda bi, h, j: (bi, j, 2 * qk_w // DV_C + h))
    per_chunk = lambda: pl.BlockSpec((None, None, 2, rows // CHUNK, 1, LANES), lambda bi, h, j: (bi, h, 0, j, 0, 0))
    out = lambda w: (jax.ShapeDtypeStruct((b, H_C, 2, nt, w), F32),
                     pl.BlockSpec((None, None, 2, rows, w), lambda bi, h, j: (bi, h, 0, j, 0)))
    (qs_s, qs_b), (num_s, num_b), (ke_s, ke_b), (st_s, st_b) = out(DQK_C), out(DV_C), out(DQK_C), out(2 * LANES)
    return pl.pallas_call(
        functools.partial(_mlstm_intra_kernel, rows=rows),
        grid=(b, H_C, nt // rows),
        in_specs=[q_spec, k_spec, v_spec, pl.BlockSpec((None, rows, LANES), lambda bi, h, j: (bi, j, 0)),
                  pl.BlockSpec((None, 2, rows // CHUNK, 1, LANES), lambda bi, h, j: (bi, 0, j, 0, 0))],
        out_specs=[qs_b, num_b, ke_b, st_b, per_chunk()],
        out_shape=[qs_s, num_s, ke_s, st_s, jax.ShapeDtypeStruct((b, H_C, 2, nch, 1, LANES), F32)],
        compiler_params=_params(("parallel", "parallel", "parallel")),
        name="mlstm_intra",
    )(p, p, p, gates, m0)


def _mlstm_scan_kernel(*refs, n_lat_seg, n_ctx_seg):
    dir_refs = (refs[0:6], refs[6:12])
    o_ref, c_ref = refs[12:]
    j = pl.program_id(1)
    fwd_seg, bwd_seg = _segment_orders(n_lat_seg, n_ctx_seg)
    seg_row0 = (pl.multiple_of(fwd_seg(j) * SCAN_SEG, SCAN_SEG), pl.multiple_of(bwd_seg(j) * SCAN_SEG, SCAN_SEG))

    @pl.when(j == 0)
    def _():
        c_ref[...] = jnp.zeros_like(c_ref)
        o_ref[...] = jnp.zeros_like(o_ref)

    ones = jnp.ones((CHUNK, LANES), F32)
    nsub = SCAN_SEG // CHUNK
    for sub in range(nsub):
        chains = []
        for d in range(2):
            c = sub if d == 0 else nsub - 1 - sub
            rows = slice(c * CHUNK, (c + 1) * CHUNK)
            qs_ref = dir_refs[d][1]
            for h in range(H_C):
                c_st = c_ref[d, h]
                chains.append((d, h, c, rows, c_st, _dot(qs_ref[h, rows, :], c_st)))
        for d, h, c, rows, c_st, q_c in chains:
            v_ref, _, num_ref, ke_ref, st_ref, so_ref = dir_refs[d]
            num = q_c[:, :DV_C] + num_ref[h, rows, :]
            den = q_c[:, DV_C:] + st_ref[h, rows, 0:LANES]
            inv = 1.0 / jnp.maximum(jnp.abs(den), st_ref[h, rows, LANES:2 * LANES])
            out_rows = pl.ds(seg_row0[d] + c * CHUNK, CHUNK)
            o_ref[out_rows, h * DV_C:(h + 1) * DV_C] += num * jnp.concatenate([inv] * (DV_C // LANES), axis=1)
            decay = jnp.concatenate([so_ref[h, c]] * (DV_C // LANES + 1), axis=1)
            v_ones = jnp.concatenate([v_ref[rows, h * DV_C:(h + 1) * DV_C], ones], axis=1)
            c_ref[d, h] = c_st * decay + _dot_tn(ke_ref[h, rows, :], v_ones)


def _mlstm_scan(p, qs, num, ke, st, so, n_lat):
    b, nt, _ = p.shape
    qk_w, val_w = H_C * DQK_C, H_C * DV_C
    nsub = SCAN_SEG // CHUNK
    n_lat_seg, n_ctx_seg = n_lat // SCAN_SEG, (nt - n_lat) // SCAN_SEG
    orders = _segment_orders(n_lat_seg, n_ctx_seg)
    in_specs = []
    for d in range(2):
        seg = orders[d]
        per_head = lambda w, d=d, seg=seg: pl.BlockSpec((None, H_C, None, SCAN_SEG, w),
                                                        lambda bi, j: (bi, 0, d, seg(j), 0))
        in_specs += [pl.BlockSpec((None, SCAN_SEG, val_w), lambda bi, j, seg=seg: (bi, seg(j), 2 * qk_w // val_w)),
                     per_head(DQK_C), per_head(DV_C), per_head(DQK_C), per_head(2 * LANES),
                     pl.BlockSpec((None, H_C, None, nsub, 1, LANES), lambda bi, j, d=d, seg=seg: (bi, 0, d, seg(j), 0, 0))]
    return pl.pallas_call(
        functools.partial(_mlstm_scan_kernel, n_lat_seg=n_lat_seg, n_ctx_seg=n_ctx_seg),
        grid=(b, nt // SCAN_SEG),
        in_specs=in_specs,
        out_specs=pl.BlockSpec((None, nt, val_w), lambda bi, j: (bi, 0, 0)),
        out_shape=jax.ShapeDtypeStruct((b, nt, val_w), F32),
        scratch_shapes=[pltpu.VMEM((2, H_C, DQK_C, DV_C + LANES), F32)],
        compiler_params=_params(("parallel", "arbitrary")),
        name="mlstm_scan",
    )(p, qs, num, ke, st, so, p, qs, num, ke, st, so)


def _mlstm(xs, modtab, layer, n_lat, w_in, gate_bias):
    main = 2 * H_C * DQK_C + 2 * H_C * DV_C
    w_main = w_in[:, :main].astype(BF16)
    w_gate = jnp.pad(w_in[:, main:], ((0, 0), (0, LANES - 4 * H_C))).astype(BF16)
    p = _inproj(xs, modtab, layer, w_main, n_lat)
    gp = _inproj(xs, modtab, layer, w_gate, n_lat)
    gates, m0 = _mlstm_gates(gp, gate_bias, n_lat)
    qs, num, ke, st, so = _mlstm_intra(p, gates, m0)
    return _mlstm_scan(p, qs, num, ke, st, so, n_lat), p


def kernel(x, c, ctx, c_ctx, w_mod, b_mod, ln_g, ln_b, w_in_a, conv_a, a_log_a, dt_bias_a, norm_a, w_out_a,
           w_in_b, lam_b, subln_b, w_out_b, w_in_c, gate_bias_c, norm_c, w_out_c, w1, w2):
    b, n_lat, d = x.shape
    nt = n_lat + ctx.shape[1]
    depth = w_mod.shape[0]
    xs = jnp.concatenate([x, ctx], axis=1)
    cond_rows = -(-(b + 1) // 8) * 8
    cc = jnp.concatenate([c, c_ctx[None, :], jnp.zeros((cond_rows - b - 1, d), F32)], axis=0)
    modtab = _mod_tables(cc, w_mod, b_mod)
    rope = _rope_tables(n_lat, nt)

    for i in range(depth):
        kind, j = i % N_MIXERS, i // N_MIXERS
        rows = nt if i < depth - 1 else n_lat
        if kind == 0:
            o, p = _gated_deltanet(xs, modtab, i, n_lat, w_in_a[j], conv_a[j], a_log_a[j], dt_bias_a[j])
            xs_new = _post(xs, modtab, i, o, p, 3, jnp.tile(norm_a[j], H_A), w_out_a[j].astype(BF16),
                           ln_g[i, 0], ln_b[i, 0], n_lat, rows, DK_A, "silu")
        elif kind == 1:
            lam_init = 0.8 - 0.6 * math.exp(-0.3 * i)
            p = _inproj(xs, modtab, i, w_in_b[j].astype(BF16), n_lat, rope=rope, rope_blocks=2, out_dtype=BF16)
            o = _diff_attention(p, lam_b[j], n_lat, lam_init)
            xs_new = _post(xs, modtab, i, o, None, 0, jnp.tile(subln_b[j], H_B), w_out_b[j].astype(BF16),
                           ln_g[i, 0], ln_b[i, 0], n_lat, rows, 2 * DH_B, None, out_scale=1.0 - lam_init)
        else:
            o, p = _mlstm(xs, modtab, i, n_lat, w_in_c[j], gate_bias_c[j])
            xs_new = _post(xs, modtab, i, o, p, 2, norm_c[j], w_out_c[j].astype(BF16),
                           ln_g[i, 0], ln_b[i, 0], n_lat, rows, DV_C, "sigmoid")
        xs = _mlp(xs_new, modtab, i, w1[i].astype(BF16), w2[i].astype(BF16), ln_g[i, 1], ln_b[i, 1], n_lat)
    return xs
```

```python
import functools
import math

import jax
import jax.numpy as jnp
from jax import lax
from jax.experimental import pallas as pl
from jax.experimental.pallas import tpu as pltpu

F32 = jnp.float32
BF16 = jnp.bfloat16
HIGHEST = lax.Precision.HIGHEST

DEPTH = 4
N_MIXERS = 3
ALPHA = (2.0 * DEPTH) ** 0.25
EPS = 1e-6
LN_EPS = 1e-5
GRID_W = 64
ROPE_BASE = 10000.0
LOG2_E = math.log2(math.e)
CHUNK = 64
SCAN_SEG = 256
ATTN_QUERY_ROWS = 128
ATTN_KEY_BLOCK = 256
LANES = 128
SUBLANES = 8

H_A, DK_A = 8, 128
CONV_K = 5
H_B, DH_B = 8, 64
H_C, DQK_C, DV_C = 4, 128, 256

VMEM_LIMIT = 56 * 1024 * 1024


def _params(sem, vmem=VMEM_LIMIT):
    return pltpu.CompilerParams(dimension_semantics=sem, vmem_limit_bytes=vmem)


def _sigmoid(x):
    return 1.0 / (1.0 + jnp.exp(-x))


def _softplus(x):
    return jnp.maximum(x, 0.0) + jnp.log1p(jnp.exp(-jnp.abs(x)))


def _dot(a, b):
    return jnp.dot(a, b, preferred_element_type=F32)


def _dot_nt(a, b):
    return lax.dot_general(a, b, (((1,), (1,)), ((), ())), preferred_element_type=F32)


def _dot_tn(a, b):
    return lax.dot_general(a, b, (((0,), (0,)), ((), ())), preferred_element_type=F32)


def _row_tile(nt):
    for tm in (768, 512, 256):
        if nt % tm == 0:
            return tm
    raise ValueError(f"sequence length {nt} must be a multiple of 256")


def _mod_kernel(c_ref, w_ref, b_ref, o_ref):
    cc = c_ref[...]
    s = cc * _sigmoid(cc)
    o_ref[...] = jnp.dot(s, w_ref[...], precision=HIGHEST, preferred_element_type=F32) + b_ref[...]


def _mod_tables(cc, w_mod, b_mod):
    r, d = cc.shape
    depth, _, f = w_mod.shape
    tn = 1536
    out = pl.pallas_call(
        _mod_kernel,
        grid=(depth, f // tn),
        in_specs=[pl.BlockSpec((r, d), lambda l, n: (0, 0)),
                  pl.BlockSpec((None, d, tn), lambda l, n: (l, 0, n)),
                  pl.BlockSpec((None, 1, tn), lambda l, n: (l, 0, n))],
        out_specs=pl.BlockSpec((None, r, tn), lambda l, n: (l, 0, n)),
        out_shape=jax.ShapeDtypeStruct((depth, r, f), F32),
        compiler_params=_params(("parallel", "parallel")),
        name="mod_tables",
    )(cc, w_mod, b_mod.reshape(depth, 1, f))
    return out.reshape(depth, r, 6, d)


def _mod_specs(layer, ctx_row):
    def lat(b, *_):
        return (layer, b, 0, 0)

    def ctx(b, *_):
        return (layer, ctx_row, 0, 0)
    return lat, ctx


def _select_mod(ml_ref, mc_ref, idx, is_ctx):
    return jnp.where(is_ctx, mc_ref[idx:idx + 1, :], ml_ref[idx:idx + 1, :])


def _inproj_kernel(x_ref, ml_ref, mc_ref, w_ref, *rest, tm, n_lat, rope_blocks):
    if rope_blocks:
        cos_ref, sin_ref, o_ref, h_ref = rest
    else:
        o_ref, h_ref = rest
    j = pl.program_id(1)
    n = pl.program_id(2)

    @pl.when(n == 0)
    def _():
        row = j * tm + lax.broadcasted_iota(jnp.int32, (tm, 1), 0)
        is_ctx = row >= n_lat
        shift = _select_mod(ml_ref, mc_ref, 0, is_ctx)
        scale = _select_mod(ml_ref, mc_ref, 1, is_ctx)
        h_ref[...] = (x_ref[...] * (1.0 + scale) + shift).astype(h_ref.dtype)

    acc = _dot(h_ref[...], w_ref[...])
    if not rope_blocks:
        o_ref[...] = acc.astype(o_ref.dtype)
        return

    @pl.when(n >= rope_blocks)
    def _():
        o_ref[...] = acc.astype(o_ref.dtype)

    @pl.when(n < rope_blocks)
    def _():
        q_scale = jnp.where(n == 0, (DH_B ** -0.5) * LOG2_E, 1.0)
        cos = cos_ref[...] * q_scale
        sin = sin_ref[...] * q_scale
        lane = lax.broadcasted_iota(jnp.int32, (1, LANES), 1)
        first_half = (lane & 31) < 16
        for i in range(acc.shape[1] // LANES):
            xb = acc[:, i * LANES:(i + 1) * LANES]
            partner = jnp.where(first_half, pltpu.roll(xb, LANES - 16, axis=1), pltpu.roll(xb, 16, axis=1))
            o_ref[:, i * LANES:(i + 1) * LANES] = (xb * cos + partner * sin).astype(o_ref.dtype)


def _inproj(xs, modtab, layer, w, n_lat, rope=None, rope_blocks=0, out_dtype=F32):
    b, nt, d = xs.shape
    f = w.shape[1]
    tm = _row_tile(nt)
    tn = min(f, 1024)
    lat, ctx = _mod_specs(layer, b)
    in_specs = [pl.BlockSpec((None, tm, d), lambda bi, j, n: (bi, j, 0)),
                pl.BlockSpec((None, None, 6, d), lat),
                pl.BlockSpec((None, None, 6, d), ctx),
                pl.BlockSpec((d, tn), lambda bi, j, n: (0, n))]
    args = [xs, modtab, modtab, w]
    if rope_blocks:
        in_specs += [pl.BlockSpec((tm, LANES), lambda bi, j, n: (j, 0))] * 2
        args += list(rope)
    return pl.pallas_call(
        functools.partial(_inproj_kernel, tm=tm, n_lat=n_lat, rope_blocks=rope_blocks),
        grid=(b, nt // tm, f // tn),
        in_specs=in_specs,
        out_specs=pl.BlockSpec((None, tm, tn), lambda bi, j, n: (bi, j, n)),
        out_shape=jax.ShapeDtypeStruct((b, nt, f), out_dtype),
        scratch_shapes=[pltpu.VMEM((tm, d), BF16)],
        compiler_params=_params(("parallel", "parallel", "arbitrary")),
        name="inproj",
    )(*args)


def _layer_norm(v, g, b):
    mu = jnp.mean(v, axis=-1, keepdims=True)
    vc = v - mu
    var = jnp.mean(vc * vc, axis=-1, keepdims=True)
    return vc * lax.rsqrt(var + LN_EPS) * g + b


def _post_kernel(x_ref, ml_ref, mc_ref, o_ref, *rest, tm, n_lat, group, gate, out_scale):
    if gate is None:
        gn_ref, w_ref, lg_ref, lb_ref, out_ref = rest
    else:
        z_ref, gn_ref, w_ref, lg_ref, lb_ref, out_ref = rest
    j = pl.program_id(1)
    row = j * tm + lax.broadcasted_iota(jnp.int32, (tm, 1), 0)
    is_ctx = row >= n_lat
    width = o_ref.shape[-1]
    parts = []
    for g0 in range(0, width, group):
        og = o_ref[:, g0:g0 + group]
        ms = jnp.mean(og * og, axis=-1, keepdims=True)
        yg = og * lax.rsqrt(ms + EPS) * gn_ref[:, g0:g0 + group]
        if gate == "silu":
            z = z_ref[:, g0:g0 + group]
            yg = yg * (z * _sigmoid(z))
        elif gate == "sigmoid":
            yg = yg * _sigmoid(z_ref[:, g0:g0 + group])
        else:
            yg = yg * out_scale
        parts.append(yg.astype(BF16))
    y = _dot(jnp.concatenate(parts, axis=-1), w_ref[...])
    res_gate = _select_mod(ml_ref, mc_ref, 2, is_ctx)
    out_ref[...] = _layer_norm(ALPHA * x_ref[...] + res_gate * y, lg_ref[...], lb_ref[...])


def _post(xs, modtab, layer, o, z_arr, z_block, gnorm, w_out, ln_g, ln_b, n_lat, rows, group, gate, out_scale=1.0):
    b, _, d = xs.shape
    width = o.shape[-1]
    tm = _row_tile(rows)
    lat, ctx = _mod_specs(layer, b)
    in_specs = [pl.BlockSpec((None, tm, d), lambda bi, j: (bi, j, 0)),
                pl.BlockSpec((None, None, 6, d), lat),
                pl.BlockSpec((None, None, 6, d), ctx),
                pl.BlockSpec((None, tm, width), lambda bi, j: (bi, j, 0))]
    args = [xs, modtab, modtab, o]
    if gate is not None:
        in_specs.append(pl.BlockSpec((None, tm, width), lambda bi, j: (bi, j, z_block)))
        args.append(z_arr)
    in_specs += [pl.BlockSpec((1, width), lambda bi, j: (0, 0)),
                 pl.BlockSpec((width, d), lambda bi, j: (0, 0)),
                 pl.BlockSpec((1, d), lambda bi, j: (0, 0)),
                 pl.BlockSpec((1, d), lambda bi, j: (0, 0))]
    args += [gnorm.reshape(1, width), w_out, ln_g.reshape(1, d), ln_b.reshape(1, d)]
    return pl.pallas_call(
        functools.partial(_post_kernel, tm=tm, n_lat=n_lat, group=group, gate=gate, out_scale=out_scale),
        grid=(b, rows // tm),
        in_specs=in_specs,
        out_specs=pl.BlockSpec((None, tm, d), lambda bi, j: (bi, j, 0)),
        out_shape=jax.ShapeDtypeStruct((b, rows, d), F32),
        compiler_params=_params(("parallel", "parallel")),
        name="mixer_out",
    )(*args)


def _mlp_kernel(x_ref, ml_ref, mc_ref, w1_ref, w2_ref, lg_ref, lb_ref, out_ref, h_ref, acc_ref, *, tm, n_lat):
    j = pl.program_id(1)
    k = pl.program_id(2)
    row = j * tm + lax.broadcasted_iota(jnp.int32, (tm, 1), 0)
    is_ctx = row >= n_lat

    @pl.when(k == 0)
    def _():
        shift = _select_mod(ml_ref, mc_ref, 3, is_ctx)
        scale = _select_mod(ml_ref, mc_ref, 4, is_ctx)
        h_ref[...] = (x_ref[...] * (1.0 + scale) + shift).astype(h_ref.dtype)
        acc_ref[...] = jnp.zeros_like(acc_ref)

    a = jnp.maximum(_dot(h_ref[...], w1_ref[...]), 0.0)
    acc_ref[...] += _dot((a * a).astype(BF16), w2_ref[...])

    @pl.when(k == pl.num_programs(2) - 1)
    def _():
        res_gate = _select_mod(ml_ref, mc_ref, 5, is_ctx)
        out_ref[...] = _layer_norm(ALPHA * x_ref[...] + res_gate * acc_ref[...], lg_ref[...], lb_ref[...])


def _mlp(xs, modtab, layer, w1, w2, ln_g, ln_b, n_lat):
    b, rows, d = xs.shape
    dff = w1.shape[1]
    tm = _row_tile(rows)
    tf = 1024
    lat, ctx = _mod_specs(layer, b)
    return pl.pallas_call(
        functools.partial(_mlp_kernel, tm=tm, n_lat=n_lat),
        grid=(b, rows // tm, dff // tf),
        in_specs=[pl.BlockSpec((None, tm, d), lambda bi, j, k: (bi, j, 0)),
                  pl.BlockSpec((None, None, 6, d), lat),
                  pl.BlockSpec((None, None, 6, d), ctx),
                  pl.BlockSpec((d, tf), lambda bi, j, k: (0, k)),
                  pl.BlockSpec((tf, d), lambda bi, j, k: (k, 0)),
                  pl.BlockSpec((1, d), lambda bi, j, k: (0, 0)),
                  pl.BlockSpec((1, d), lambda bi, j, k: (0, 0))],
        out_specs=pl.BlockSpec((None, tm, d), lambda bi, j, k: (bi, j, 0)),
        out_shape=jax.ShapeDtypeStruct((b, rows, d), F32),
        scratch_shapes=[pltpu.VMEM((tm, d), BF16), pltpu.VMEM((tm, d), F32)],
        compiler_params=_params(("parallel", "parallel", "arbitrary")),
        name="mlp",
    )(xs, modtab, modtab, w1, w2, ln_g.reshape(1, d), ln_b.reshape(1, d))


def _segment_orders(n_lat_seg, n_ctx_seg):
    def fwd(j):
        return jnp.where(j < n_ctx_seg, j + n_lat_seg, j - n_ctx_seg)

    def bwd(j):
        return n_lat_seg + n_ctx_seg - 1 - j
    return fwd, bwd


def _cumsum_mats():
    r = lax.broadcasted_iota(jnp.int32, (CHUNK, CHUNK), 0)
    c = lax.broadcasted_iota(jnp.int32, (CHUNK, CHUNK), 1)
    return (c <= r).astype(F32), (c >= r).astype(F32)


def _chunk_cumsums(val, fwd_cols, bwd_cols, rows):
    lower, upper = _cumsum_mats()
    out = []
    for r0 in range(0, rows, CHUNK):
        v = val[r0:r0 + CHUNK]
        cf = jnp.dot(lower, v, precision=HIGHEST, preferred_element_type=F32)
        cb = jnp.dot(upper, v, precision=HIGHEST, preferred_element_type=F32)
        out.append(jnp.where(fwd_cols, cf, jnp.where(bwd_cols, cb, v)))
    return jnp.concatenate(out, axis=0)


def _pick_col(tile, idx):
    lane = lax.broadcasted_iota(jnp.int32, tile.shape, 1)
    return jnp.sum(jnp.where(lane == idx, tile, 0.0), axis=1, keepdims=True)


def _pick_row(tile, idx):
    sub = lax.broadcasted_iota(jnp.int32, tile.shape, 0)
    return jnp.sum(jnp.where(sub == idx, tile, 0.0), axis=0, keepdims=True)


def _tri_masks(direction):
    r = lax.broadcasted_iota(jnp.int32, (CHUNK, CHUNK), 0)
    c = lax.broadcasted_iota(jnp.int32, (CHUNK, CHUNK), 1)
    if direction == 0:
        return c <= r, c < r
    return c >= r, c > r


def _unit_tri_inverses(mats):
    r = lax.broadcasted_iota(jnp.int32, (CHUNK, CHUNK), 0)
    c = lax.broadcasted_iota(jnp.int32, (CHUNK, CHUNK), 1)
    differ = r ^ c
    eye = jnp.where(r == c, 1.0, 0.0)
    ts = [eye - jnp.where(differ < 2, a, 0.0) for a in mats]
    size = 2
    while size < CHUNK:
        joins = (differ >= size) & (differ < 2 * size)
        ets = [_dot(jnp.where(joins, a, 0.0), t) for a, t in zip(mats, ts)]
        ts = [t - _dot(t, et) for t, et in zip(ts, ets)]
        size *= 2
    return ts


def _gdn_gate_kernel(gp_ref, alog_ref, dtb_ref, o_ref, *, rows):
    x = gp_ref[...]
    lane = lax.broadcasted_iota(jnp.int32, (1, LANES), 1)
    live = lane < 4 * H_A
    is_decay = live & ((lane & (2 * H_A - 1)) < H_A)
    g = -jnp.exp(alog_ref[...]) * _softplus(x + dtb_ref[...])
    val = jnp.where(is_decay, g, _sigmoid(x))
    o_ref[...] = _chunk_cumsums(val, is_decay & (lane < 2 * H_A), is_decay & (lane >= 2 * H_A), rows)


def _gdn_gates(gp, a_log, dt_bias):
    b, nt, _ = gp.shape
    rows = 256
    pad = lambda t: jnp.pad(jnp.stack([t[0], jnp.zeros_like(t[0]), t[1], jnp.zeros_like(t[1])]).reshape(1, -1),
                            ((0, 0), (0, LANES - 4 * H_A)))
    return pl.pallas_call(
        functools.partial(_gdn_gate_kernel, rows=rows),
        grid=(b, nt // rows),
        in_specs=[pl.BlockSpec((None, rows, LANES), lambda bi, j: (bi, j, 0)),
                  pl.BlockSpec((1, LANES), lambda bi, j: (0, 0)),
                  pl.BlockSpec((1, LANES), lambda bi, j: (0, 0))],
        out_specs=pl.BlockSpec((None, rows, LANES), lambda bi, j: (bi, j, 0)),
        out_shape=jax.ShapeDtypeStruct((b, nt, LANES), F32),
        compiler_params=_params(("parallel", "parallel")),
        name="gdn_gates",
    )(gp, pad(a_log.astype(F32)), pad(dt_bias.astype(F32)))


def _gdn_conv_kernel(p_ref, w_ref, o_ref, pad_ref, *, n_lat, nt):
    half = CONV_K // 2
    width = p_ref.shape[-1]
    zeros = jnp.zeros((SUBLANES, width), F32)
    segments = ((0, n_lat, SUBLANES), (n_lat, nt, 2 * SUBLANES))
    pad_ref[0:SUBLANES, :] = zeros
    for lo, hi, off in segments:
        pad_ref[lo + off:hi + off, :] = p_ref[lo:hi, :]
        pad_ref[hi + off:hi + off + SUBLANES, :] = zeros
    for g in range(width // DK_A):
        head = pl.program_id(1) * (width // DK_A) + g
        lanes = slice(g * DK_A, (g + 1) * DK_A)
        for lo, hi, off in segments:
            acc = p_ref[lo:hi, lanes] * w_ref[half:half + 1, lanes]
            for s in range(-half, half + 1):
                if s != 0:
                    acc = acc + pad_ref[lo + off + s:hi + off + s, lanes] * w_ref[half + s:half + s + 1, lanes]
            y = acc * _sigmoid(acc)
            inv = lax.rsqrt(jnp.sum(y * y, axis=-1, keepdims=True) + EPS)
            factor = jnp.where(head < H_A, inv * (DK_A ** -0.5), jnp.where(head < 2 * H_A, inv, 1.0))
            o_ref[lo:hi, lanes] = y * factor


def _gdn_conv(p, conv_w, n_lat):
    b, nt, _ = p.shape
    width = DK_A
    nblk = 3 * H_A * DK_A // width
    return pl.pallas_call(
        functools.partial(_gdn_conv_kernel, n_lat=n_lat, nt=nt),
        grid=(b, nblk),
        in_specs=[pl.BlockSpec((None, nt, width), lambda bi, cb: (bi, 0, cb)),
                  pl.BlockSpec((CONV_K, width), lambda bi, cb: (0, cb))],
        out_specs=pl.BlockSpec((None, nt, width), lambda bi, cb: (bi, 0, cb)),
        out_shape=jax.ShapeDtypeStruct((b, nt, nblk * width), F32),
        scratch_shapes=[pltpu.VMEM((nt + 3 * SUBLANES, width), F32)],
        compiler_params=_params(("parallel", "parallel")),
        name="gdn_conv",
    )(p, conv_w.astype(F32))


def _gdn_intra_kernel(q_ref, k_ref, v_ref, g_ref, u_ref, w_ref, qg_ref, kd_ref, qk_ref, gl_ref, *, rows):
    h = pl.program_id(1)
    chains = []
    for pair in range(rows // (2 * CHUNK)):
        p0 = pair * 2 * CHUNK
        gates = g_ref[p0:p0 + 2 * CHUNK, :]
        gates_t = gates.T
        for sub in range(2):
            r0 = p0 + sub * CHUNK
            q = q_ref[r0:r0 + CHUNK, :]
            k = k_ref[r0:r0 + CHUNK, :]
            v = v_ref[r0:r0 + CHUNK, :]
            kk = _dot_nt(k, k)
            qk = _dot_nt(q, k)
            tile = gates[sub * CHUNK:(sub + 1) * CHUNK, :]
            for d in range(2):
                tri, strict = _tri_masks(d)
                gcol = _pick_col(tile, 2 * H_A * d + h)
                beta = _pick_col(tile, 2 * H_A * d + H_A + h)
                grow = _pick_row(gates_t, 2 * H_A * d + h)[:, sub * CHUNK:(sub + 1) * CHUNK]
                decay = jnp.where(tri, jnp.exp(jnp.where(tri, gcol - grow, 0.0)), 0.0)
                eg = jnp.exp(gcol)
                last = CHUNK - 1 if d == 0 else 0
                g_last = gcol[last:last + 1, :]
                qg_ref[d, r0:r0 + CHUNK, :] = q * eg
                kd_ref[d, r0:r0 + CHUNK, :] = k * jnp.exp(g_last - gcol)
                qk_ref[r0:r0 + CHUNK, d * CHUNK:(d + 1) * CHUNK] = jnp.where(tri, qk * decay, 0.0)
                gl_ref[d, r0 // CHUNK] = jnp.broadcast_to(jnp.exp(g_last), (1, LANES))
                chains.append((d, r0, jnp.where(strict, kk * beta * decay, 0.0),
                               jnp.concatenate([v * beta, k * (beta * eg)], axis=1)))
    t_invs = _unit_tri_inverses([a for _, _, a, _ in chains])
    for (d, r0, _, rhs), t_inv in zip(chains, t_invs):
        uw = _dot(t_inv, rhs)
        u_ref[d, r0:r0 + CHUNK, :] = uw[:, :DK_A]
        w_ref[d, r0:r0 + CHUNK, :] = uw[:, DK_A:]


def _gdn_intra(qkv, gates):
    b, nt, _ = qkv.shape
    rows = _row_tile(nt)
    nch = nt // CHUNK
    big = jax.ShapeDtypeStruct((b, H_A, 2, nt, DK_A), F32)
    big_spec = pl.BlockSpec((None, None, 2, rows, DK_A), lambda bi, h, j: (bi, h, 0, j, 0))
    col = lambda off: pl.BlockSpec((None, rows, DK_A), lambda bi, h, j: (bi, j, off + h))
    return pl.pallas_call(
        functools.partial(_gdn_intra_kernel, rows=rows),
        grid=(b, H_A, nt // rows),
        in_specs=[col(0), col(H_A), col(2 * H_A),
                  pl.BlockSpec((None, rows, LANES), lambda bi, h, j: (bi, j, 0))],
        out_specs=[big_spec, big_spec, big_spec, big_spec,
                   pl.BlockSpec((None, None, rows, 2 * CHUNK), lambda bi, h, j: (bi, h, j, 0)),
                   pl.BlockSpec((None, None, 2, rows // CHUNK, 1, LANES), lambda bi, h, j: (bi, h, 0, j, 0, 0))],
        out_shape=[big, big, big, big,
                   jax.ShapeDtypeStruct((b, H_A, nt, 2 * CHUNK), F32),
                   jax.ShapeDtypeStruct((b, H_A, 2, nch, 1, LANES), F32)],
        compiler_params=_params(("parallel", "parallel", "parallel")),
        name="gdn_intra",
    )(qkv, qkv, qkv, gates)


def _gdn_scan_kernel(*refs, n_lat_seg, n_ctx_seg):
    dir_refs = (refs[0:6], refs[6:12])
    o_ref, s_ref = refs[12:]
    j = pl.program_id(1)
    fwd_seg, bwd_seg = _segment_orders(n_lat_seg, n_ctx_seg)
    seg_row0 = (pl.multiple_of(fwd_seg(j) * SCAN_SEG, SCAN_SEG), pl.multiple_of(bwd_seg(j) * SCAN_SEG, SCAN_SEG))

    @pl.when(j == 0)
    def _():
        s_ref[...] = jnp.zeros_like(s_ref)
        o_ref[...] = jnp.zeros_like(o_ref)

    nsub = SCAN_SEG // CHUNK
    for sub in range(nsub):
        chains = []
        for d in range(2):
            c = sub if d == 0 else nsub - 1 - sub
            rows = slice(c * CHUNK, (c + 1) * CHUNK)
            _, w_ref, qg_ref, _, _, _ = dir_refs[d]
            for h in range(H_A):
                state = s_ref[d, h]
                wq = _dot(jnp.concatenate([w_ref[h, rows, :], qg_ref[h, rows, :]], axis=0), state)
                chains.append((d, h, c, rows, state, wq))
        for d, h, c, rows, state, wq in chains:
            u_ref, _, _, kd_ref, qk_ref, gl_ref = dir_refs[d]
            v_new = u_ref[h, rows, :] - wq[:CHUNK]
            qk = qk_ref[h, rows, d * CHUNK:(d + 1) * CHUNK]
            out_rows = pl.ds(seg_row0[d] + c * CHUNK, CHUNK)
            o_ref[out_rows, h * DK_A:(h + 1) * DK_A] += wq[CHUNK:] + _dot(qk, v_new)
            s_ref[d, h] = state * gl_ref[h, c] + _dot_tn(kd_ref[h, rows, :], v_new)


def _gdn_scan(u, w, qg, kd, qk, gl, n_lat):
    b, h, _, nt, dk = u.shape
    nsub = SCAN_SEG // CHUNK
    n_lat_seg, n_ctx_seg = n_lat // SCAN_SEG, (nt - n_lat) // SCAN_SEG
    orders = _segment_orders(n_lat_seg, n_ctx_seg)
    in_specs = []
    for d in range(2):
        seg = orders[d]
        big = pl.BlockSpec((None, h, None, SCAN_SEG, dk), lambda bi, j, d=d, seg=seg: (bi, 0, d, seg(j), 0))
        in_specs += [big, big, big, big,
                     pl.BlockSpec((None, h, SCAN_SEG, 2 * CHUNK), lambda bi, j, seg=seg: (bi, 0, seg(j), 0)),
                     pl.BlockSpec((None, h, None, nsub, 1, LANES), lambda bi, j, d=d, seg=seg: (bi, 0, d, seg(j), 0, 0))]
    return pl.pallas_call(
        functools.partial(_gdn_scan_kernel, n_lat_seg=n_lat_seg, n_ctx_seg=n_ctx_seg),
        grid=(b, nt // SCAN_SEG),
        in_specs=in_specs,
        out_specs=pl.BlockSpec((None, nt, h * dk), lambda bi, j: (bi, 0, 0)),
        out_shape=jax.ShapeDtypeStruct((b, nt, h * dk), F32),
        scratch_shapes=[pltpu.VMEM((2, h, dk, dk), F32)],
        compiler_params=_params(("parallel", "arbitrary")),
        name="gdn_scan",
    )(u, w, qg, kd, qk, gl, u, w, qg, kd, qk, gl)


def _gated_deltanet(xs, modtab, layer, n_lat, w_in, conv_w, a_log, dt_bias):
    key = H_A * DK_A
    w_main = w_in[:, :4 * key].astype(BF16)
    w_gate = jnp.pad(w_in[:, 4 * key:], ((0, 0), (0, LANES - 4 * H_A))).astype(BF16)
    p = _inproj(xs, modtab, layer, w_main, n_lat)
    gp = _inproj(xs, modtab, layer, w_gate, n_lat)
    gates = _gdn_gates(gp, a_log, dt_bias)
    qkv = _gdn_conv(p, conv_w, n_lat)
    u, w, qg, kd, qk, gl = _gdn_intra(qkv, gates)
    o = _gdn_scan(u, w, qg, kd, qk, gl, n_lat)
    return o, p


def _rope_tables(n_lat, nt):
    rows = n_lat // GRID_W
    r = jnp.repeat(jnp.arange(rows, dtype=F32), GRID_W)
    col = jnp.tile(jnp.arange(GRID_W, dtype=F32), rows)
    half = DH_B // 2
    inv = ROPE_BASE ** (-jnp.arange(0, half, 2, dtype=F32) / half)
    ang_r, ang_c = r[:, None] * inv, col[:, None] * inv
    cos = jnp.concatenate([jnp.cos(ang_r)] * 2 + [jnp.cos(ang_c)] * 2, axis=-1)
    sin = jnp.concatenate([-jnp.sin(ang_r), jnp.sin(ang_r), -jnp.sin(ang_c), jnp.sin(ang_c)], axis=-1)
    cos = jnp.concatenate([jnp.tile(cos, (1, LANES // DH_B)), jnp.ones((nt - n_lat, LANES), F32)], axis=0)
    sin = jnp.concatenate([jnp.tile(sin, (1, LANES // DH_B)), jnp.zeros((nt - n_lat, LANES), F32)], axis=0)
    return cos, sin


def _attn_kernel(q_ref, k_ref, v_ref, lam_ref, o_ref, *, tq, n_lat, nt, lam_init):
    j = pl.program_id(2)
    lam = lam_ref[...]
    lam_full = (jnp.exp(jnp.sum(lam[0:1] * lam[1:2], axis=-1, keepdims=True))
                - jnp.exp(jnp.sum(lam[2:3] * lam[3:4], axis=-1, keepdims=True)) + lam_init)
    dv = 2 * DH_B
    lane = lax.broadcasted_iota(jnp.int32, (1, dv), 1)

    def attend(key_lo, key_hi):
        block = min(ATTN_KEY_BLOCK, key_hi - key_lo)
        ones = jnp.ones((block, dv), BF16)
        for r0 in range(0, tq, ATTN_QUERY_ROWS):
            q = q_ref[r0:r0 + ATTN_QUERY_ROWS, :]
            q_maps = jnp.concatenate([jnp.where(lane < DH_B, q, 0), jnp.where(lane >= DH_B, q, 0)], axis=0)
            m = acc = None
            for k0 in range(key_lo, key_hi, block):
                s = _dot_nt(q_maps, k_ref[k0:k0 + block, :])
                v_ext = jnp.concatenate([v_ref[k0:k0 + block, :], ones], axis=1)
                s_max = jnp.max(s, axis=-1, keepdims=True)
                if m is None:
                    m = s_max
                    acc = _dot(jnp.exp2(s - m).astype(BF16), v_ext)
                else:
                    m_new = jnp.maximum(m, s_max)
                    acc = jnp.exp2(m - m_new) * acc + _dot(jnp.exp2(s - m_new).astype(BF16), v_ext)
                    m = m_new
            normalised = acc[:, :dv] / acc[:, dv:]
            o_ref[r0:r0 + ATTN_QUERY_ROWS, :] = (normalised[:ATTN_QUERY_ROWS]
                                                 - lam_full * normalised[ATTN_QUERY_ROWS:])

    @pl.when(j * tq < n_lat)
    def _():
        attend(0, nt)

    @pl.when(j * tq >= n_lat)
    def _():
        attend(n_lat, nt)


def _diff_attention(p, lam, n_lat, lam_init):
    b, nt, _ = p.shape
    tq = 256
    dv = 2 * DH_B
    return pl.pallas_call(
        functools.partial(_attn_kernel, tq=tq, n_lat=n_lat, nt=nt, lam_init=lam_init),
        grid=(b, H_B, nt // tq),
        in_specs=[pl.BlockSpec((None, tq, dv), lambda bi, h, j: (bi, j, h)),
                  pl.BlockSpec((None, nt, dv), lambda bi, h, j: (bi, 0, H_B + h)),
                  pl.BlockSpec((None, nt, dv), lambda bi, h, j: (bi, 0, 2 * H_B + h)),
                  pl.BlockSpec((4, DH_B), lambda bi, h, j: (0, 0))],
        out_specs=pl.BlockSpec((None, tq, dv), lambda bi, h, j: (bi, j, h)),
        out_shape=jax.ShapeDtypeStruct((b, nt, H_B * dv), F32),
        compiler_params=_params(("parallel", "parallel", "arbitrary")),
        name="diff_attn",
    )(p, p, p, lam.astype(F32))


def _mlstm_gate_kernel(gp_ref, bias_ref, o_ref, m0_ref, *, nt, n_lat):
    x = gp_ref[...] + bias_ref[...]
    lane = lax.broadcasted_iota(jnp.int32, (1, LANES), 1)
    live = lane < 4 * H_C
    is_forget = live & ((lane & H_C) != 0)
    fwd_forget = is_forget & (lane < 2 * H_C)
    val = jnp.where(is_forget, -_softplus(-x), x)
    gates = _chunk_cumsums(val, fwd_forget, is_forget & (lane >= 2 * H_C), nt)
    o_ref[...] = gates
    nch, n_lat_ch = nt // CHUNK, n_lat // CHUNK
    b_last, m_end = [], []
    for c in range(nch):
        tile = gates[c * CHUNK:(c + 1) * CHUNK, :]
        total = jnp.where(fwd_forget, tile[CHUNK - 1:CHUNK, :], tile[0:1, :])
        a_end = total - tile + pltpu.roll(tile, H_C, axis=1)
        b_last.append(total)
        m_end.append(jnp.max(a_end, axis=0, keepdims=True))
    orders = (list(range(n_lat_ch, nch)) + list(range(n_lat_ch)), list(range(nch - 1, -1, -1)))
    for d in range(2):
        m = jnp.zeros((1, LANES), F32)
        for c in orders[d]:
            m0_ref[d, c] = m
            m = jnp.maximum(b_last[c] + m, m_end[c])


def _mlstm_gates(gp, gate_bias, n_lat):
    b, nt, _ = gp.shape
    nch = nt // CHUNK
    bias = jnp.pad(gate_bias.astype(F32).reshape(1, -1), ((0, 0), (0, LANES - 4 * H_C)))
    return pl.pallas_call(
        functools.partial(_mlstm_gate_kernel, nt=nt, n_lat=n_lat),
        grid=(b,),
        in_specs=[pl.BlockSpec((None, nt, LANES), lambda bi: (bi, 0, 0)),
                  pl.BlockSpec((1, LANES), lambda bi: (0, 0))],
        out_specs=[pl.BlockSpec((None, nt, LANES), lambda bi: (bi, 0, 0)),
                   pl.BlockSpec((None, 2, nch, 1, LANES), lambda bi: (bi, 0, 0, 0, 0))],
        out_shape=[jax.ShapeDtypeStruct((b, nt, LANES), F32),
                   jax.ShapeDtypeStruct((b, 2, nch, 1, LANES), F32)],
        compiler_params=_params(("parallel",)),
        name="mlstm_gates",
    )(gp, bias)


def _mlstm_intra_kernel(q_ref, k_ref, v_ref, g_ref, m0_ref, qs_ref, num_ref, ke_ref, st_ref, so_ref, *, rows):
    h = pl.program_id(1)
    for pair in range(rows // (2 * CHUNK)):
        p0 = pair * 2 * CHUNK
        gates = g_ref[p0:p0 + 2 * CHUNK, :]
        gates_t = gates.T
        for sub in range(2):
            r0 = p0 + sub * CHUNK
            q = q_ref[r0:r0 + CHUNK, :]
            k = k_ref[r0:r0 + CHUNK, :] * (DQK_C ** -0.5)
            v = v_ref[r0:r0 + CHUNK, :]
            qk = _dot_nt(q, k)
            tile = gates[sub * CHUNK:(sub + 1) * CHUNK, :]
            for d in range(2):
                tri, _ = _tri_masks(d)
                li_idx = 2 * d * H_C + h
                bc_idx = (2 * d + 1) * H_C + h
                bcol = _pick_col(tile, bc_idx)
                licol = _pick_col(tile, li_idx)
                brow = _pick_row(gates_t, bc_idx)[:, sub * CHUNK:(sub + 1) * CHUNK]
                lirow = _pick_row(gates_t, li_idx)[:, sub * CHUNK:(sub + 1) * CHUNK]
                m0 = _pick_col(m0_ref[d, r0 // CHUNK], bc_idx)
                dmat = jnp.where(tri, bcol - brow + lirow, -jnp.inf)
                m_intra = jnp.max(dmat, axis=-1, keepdims=True)
                m_t = jnp.maximum(bcol + m0, m_intra)
                wts = jnp.exp(dmat - m_t) * qk
                last = CHUNK - 1 if d == 0 else 0
                b_last = bcol[last:last + 1, :]
                m_new = jnp.maximum(b_last + m0, m_intra[last:last + 1, :])
                qs_ref[d, r0:r0 + CHUNK, :] = q * jnp.exp(bcol + m0 - m_t)
                num_ref[d, r0:r0 + CHUNK, :] = _dot(wts, v)
                ke_ref[d, r0:r0 + CHUNK, :] = k * jnp.exp(b_last - bcol + licol - m_new)
                den = jnp.sum(wts, axis=-1, keepdims=True)
                st_ref[d, r0:r0 + CHUNK, 0:LANES] = jnp.broadcast_to(den, (CHUNK, LANES))
                st_ref[d, r0:r0 + CHUNK, LANES:2 * LANES] = jnp.broadcast_to(jnp.exp(-m_t), (CHUNK, LANES))
                so_ref[d, r0 // CHUNK] = jnp.broadcast_to(jnp.exp(b_last + m0 - m_new), (1, LANES))


def _mlstm_intra(p, gates, m0):
    b, nt, _ = p.shape
    rows = 256
    qk_w = H_C * DQK_C
    nch = nt // CHUNK
    q_spec = pl.BlockSpec((None, rows, DQK_C), lambda bi, h, j: (bi, j, h))
    k_spec = pl.BlockSpec((None, rows, DQK_C), lambda bi, h, j: (bi, j, H_C + h))
    v_spec = pl.BlockSpec((None, rows, DV_C), lambda bi, h, j: (bi, j, 2 * qk_w // DV_C + h))
    per_chunk = lambda: pl.BlockSpec((None, None, 2, rows // CHUNK, 1, LANES), lambda bi, h, j: (bi, h, 0, j, 0, 0))
    out = lambda w: (jax.ShapeDtypeStruct((b, H_C, 2, nt, w), F32),
                     pl.BlockSpec((None, None, 2, rows, w), lambda bi, h, j: (bi, h, 0, j, 0)))
    (qs_s, qs_b), (num_s, num_b), (ke_s, ke_b), (st_s, st_b) = out(DQK_C), out(DV_C), out(DQK_C), out(2 * LANES)
    return pl.pallas_call(
        functools.partial(_mlstm_intra_kernel, rows=rows),
        grid=(b, H_C, nt // rows),
        in_specs=[q_spec, k_spec, v_spec, pl.BlockSpec((None, rows, LANES), lambda bi, h, j: (bi, j, 0)),
                  pl.BlockSpec((None, 2, rows // CHUNK, 1, LANES), lambda bi, h, j: (bi, 0, j, 0, 0))],
        out_specs=[qs_b, num_b, ke_b, st_b, per_chunk()],
        out_shape=[qs_s, num_s, ke_s, st_s, jax.ShapeDtypeStruct((b, H_C, 2, nch, 1, LANES), F32)],
        compiler_params=_params(("parallel", "parallel", "parallel")),
        name="mlstm_intra",
    )(p, p, p, gates, m0)


def _mlstm_scan_kernel(*refs, n_lat_seg, n_ctx_seg):
    dir_refs = (refs[0:6], refs[6:12])
    o_ref, c_ref = refs[12:]
    j = pl.program_id(1)
    fwd_seg, bwd_seg = _segment_orders(n_lat_seg, n_ctx_seg)
    seg_row0 = (pl.multiple_of(fwd_seg(j) * SCAN_SEG, SCAN_SEG), pl.multiple_of(bwd_seg(j) * SCAN_SEG, SCAN_SEG))

    @pl.when(j == 0)
    def _():
        c_ref[...] = jnp.zeros_like(c_ref)
        o_ref[...] = jnp.zeros_like(o_ref)

    ones = jnp.ones((CHUNK, LANES), F32)
    nsub = SCAN_SEG // CHUNK
    for sub in range(nsub):
        chains = []
        for d in range(2):
            c = sub if d == 0 else nsub - 1 - sub
            rows = slice(c * CHUNK, (c + 1) * CHUNK)
            qs_ref = dir_refs[d][1]
            for h in range(H_C):
                c_st = c_ref[d, h]
                chains.append((d, h, c, rows, c_st, _dot(qs_ref[h, rows, :], c_st)))
        for d, h, c, rows, c_st, q_c in chains:
            v_ref, _, num_ref, ke_ref, st_ref, so_ref = dir_refs[d]
            num = q_c[:, :DV_C] + num_ref[h, rows, :]
            den = q_c[:, DV_C:] + st_ref[h, rows, 0:LANES]
            inv = 1.0 / jnp.maximum(jnp.abs(den), st_ref[h, rows, LANES:2 * LANES])
            out_rows = pl.ds(seg_row0[d] + c * CHUNK, CHUNK)
            o_ref[out_rows, h * DV_C:(h + 1) * DV_C] += num * jnp.concatenate([inv] * (DV_C // LANES), axis=1)
            decay = jnp.concatenate([so_ref[h, c]] * (DV_C // LANES + 1), axis=1)
            v_ones = jnp.concatenate([v_ref[rows, h * DV_C:(h + 1) * DV_C], ones], axis=1)
            c_ref[d, h] = c_st * decay + _dot_tn(ke_ref[h, rows, :], v_ones)


def _mlstm_scan(p, qs, num, ke, st, so, n_lat):
    b, nt, _ = p.shape
    qk_w, val_w = H_C * DQK_C, H_C * DV_C
    nsub = SCAN_SEG // CHUNK
    n_lat_seg, n_ctx_seg = n_lat // SCAN_SEG, (nt - n_lat) // SCAN_SEG
    orders = _segment_orders(n_lat_seg, n_ctx_seg)
    in_specs = []
    for d in range(2):
        seg = orders[d]
        per_head = lambda w, d=d, seg=seg: pl.BlockSpec((None, H_C, None, SCAN_SEG, w),
                                                        lambda bi, j: (bi, 0, d, seg(j), 0))
        in_specs += [pl.BlockSpec((None, SCAN_SEG, val_w), lambda bi, j, seg=seg: (bi, seg(j), 2 * qk_w // val_w)),
                     per_head(DQK_C), per_head(DV_C), per_head(DQK_C), per_head(2 * LANES),
                     pl.BlockSpec((None, H_C, None, nsub, 1, LANES), lambda bi, j, d=d, seg=seg: (bi, 0, d, seg(j), 0, 0))]
    return pl.pallas_call(
        functools.partial(_mlstm_scan_kernel, n_lat_seg=n_lat_seg, n_ctx_seg=n_ctx_seg),
        grid=(b, nt // SCAN_SEG),
        in_specs=in_specs,
        out_specs=pl.BlockSpec((None, nt, val_w), lambda bi, j: (bi, 0, 0)),
        out_shape=jax.ShapeDtypeStruct((b, nt, val_w), F32),
        scratch_shapes=[pltpu.VMEM((2, H_C, DQK_C, DV_C + LANES), F32)],
        compiler_params=_params(("parallel", "arbitrary")),
        name="mlstm_scan",
    )(p, qs, num, ke, st, so, p, qs, num, ke, st, so)


def _mlstm(xs, modtab, layer, n_lat, w_in, gate_bias):
    main = 2 * H_C * DQK_C + 2 * H_C * DV_C
    w_main = w_in[:, :main].astype(BF16)
    w_gate = jnp.pad(w_in[:, main:], ((0, 0), (0, LANES - 4 * H_C))).astype(BF16)
    p = _inproj(xs, modtab, layer, w_main, n_lat)
    gp = _inproj(xs, modtab, layer, w_gate, n_lat)
    gates, m0 = _mlstm_gates(gp, gate_bias, n_lat)
    qs, num, ke, st, so = _mlstm_intra(p, gates, m0)
    return _mlstm_scan(p, qs, num, ke, st, so, n_lat), p


def kernel(x, c, ctx, c_ctx, w_mod, b_mod, ln_g, ln_b, w_in_a, conv_a, a_log_a, dt_bias_a, norm_a, w_out_a,
           w_in_b, lam_b, subln_b, w_out_b, w_in_c, gate_bias_c, norm_c, w_out_c, w1, w2):
    b, n_lat, d = x.shape
    nt = n_lat + ctx.shape[1]
    depth = w_mod.shape[0]
    xs = jnp.concatenate([x, ctx], axis=1)
    cond_rows = -(-(b + 1) // 8) * 8
    cc = jnp.concatenate([c, c_ctx[None, :], jnp.zeros((cond_rows - b - 1, d), F32)], axis=0)
    modtab = _mod_tables(cc, w_mod, b_mod)
    rope = _rope_tables(n_lat, nt)

    for i in range(depth):
        kind, j = i % N_MIXERS, i // N_MIXERS
        rows = nt if i < depth - 1 else n_lat
        if kind == 0:
            o, p = _gated_deltanet(xs, modtab, i, n_lat, w_in_a[j], conv_a[j], a_log_a[j], dt_bias_a[j])
            xs_new = _post(xs, modtab, i, o, p, 3, jnp.tile(norm_a[j], H_A), w_out_a[j].astype(BF16),
                           ln_g[i, 0], ln_b[i, 0], n_lat, rows, DK_A, "silu")
        elif kind == 1:
            lam_init = 0.8 - 0.6 * math.exp(-0.3 * i)
            p = _inproj(xs, modtab, i, w_in_b[j].astype(BF16), n_lat, rope=rope, rope_blocks=2, out_dtype=BF16)
            o = _diff_attention(p, lam_b[j], n_lat, lam_init)
            xs_new = _post(xs, modtab, i, o, None, 0, jnp.tile(subln_b[j], H_B), w_out_b[j].astype(BF16),
                           ln_g[i, 0], ln_b[i, 0], n_lat, rows, 2 * DH_B, None, out_scale=1.0 - lam_init)
        else:
            o, p = _mlstm(xs, modtab, i, n_lat, w_in_c[j], gate_bias_c[j])
            xs_new = _post(xs, modtab, i, o, p, 2, norm_c[j], w_out_c[j].astype(BF16),
                           ln_g[i, 0], ln_b[i, 0], n_lat, rows, DV_C, "sigmoid")
        xs = _mlp(xs_new, modtab, i, w1[i].astype(BF16), w2[i].astype(BF16), ln_g[i, 1], ln_b[i, 1], n_lat)
    return xs
```

```python
import functools
import math

import jax
import jax.numpy as jnp
from jax import lax
from jax.experimental import pallas as pl
from jax.experimental.pallas import tpu as pltpu

F32 = jnp.float32
BF16 = jnp.bfloat16
HIGHEST = lax.Precision.HIGHEST

DEPTH = 4
N_MIXERS = 3
ALPHA = (2.0 * DEPTH) ** 0.25
EPS = 1e-6
LN_EPS = 1e-5
GRID_W = 64
ROPE_BASE = 10000.0
LOG2_E = math.log2(math.e)
CHUNK = 64
SCAN_SEG = 256
ATTN_QUERY_ROWS = 128
ATTN_KEY_BLOCK = 256
LANES = 128
SUBLANES = 8

H_A, DK_A = 8, 128
CONV_K = 5
H_B, DH_B = 8, 64
H_C, DQK_C, DV_C = 4, 128, 256

VMEM_LIMIT = 56 * 1024 * 1024


def _params(sem, vmem=VMEM_LIMIT):
    return pltpu.CompilerParams(dimension_semantics=sem, vmem_limit_bytes=vmem)


def _sigmoid(x):
    return 1.0 / (1.0 + jnp.exp(-x))


def _softplus(x):
    return jnp.maximum(x, 0.0) + jnp.log1p(jnp.exp(-jnp.abs(x)))


def _dot(a, b):
    return jnp.dot(a, b, preferred_element_type=F32)


def _dot_nt(a, b):
    return lax.dot_general(a, b, (((1,), (1,)), ((), ())), preferred_element_type=F32)


def _dot_tn(a, b):
    return lax.dot_general(a, b, (((0,), (0,)), ((), ())), preferred_element_type=F32)


def _row_tile(nt):
    for tm in (768, 512, 256):
        if nt % tm == 0:
            return tm
    raise ValueError(f"sequence length {nt} must be a multiple of 256")


def _mod_kernel(c_ref, w_ref, b_ref, o_ref):
    cc = c_ref[...]
    s = cc * _sigmoid(cc)
    o_ref[...] = jnp.dot(s, w_ref[...], precision=HIGHEST, preferred_element_type=F32) + b_ref[...]


def _mod_tables(cc, w_mod, b_mod):
    r, d = cc.shape
    depth, _, f = w_mod.shape
    tn = 1536
    out = pl.pallas_call(
        _mod_kernel,
        grid=(depth, f // tn),
        in_specs=[pl.BlockSpec((r, d), lambda l, n: (0, 0)),
                  pl.BlockSpec((None, d, tn), lambda l, n: (l, 0, n)),
                  pl.BlockSpec((None, 1, tn), lambda l, n: (l, 0, n))],
        out_specs=pl.BlockSpec((None, r, tn), lambda l, n: (l, 0, n)),
        out_shape=jax.ShapeDtypeStruct((depth, r, f), F32),
        compiler_params=_params(("parallel", "parallel")),
        name="mod_tables",
    )(cc, w_mod, b_mod.reshape(depth, 1, f))
    return out.reshape(depth, r, 6, d)


def _mod_specs(layer, ctx_row):
    def lat(b, *_):
        return (layer, b, 0, 0)

    def ctx(b, *_):
        return (layer, ctx_row, 0, 0)
    return lat, ctx


def _select_mod(ml_ref, mc_ref, idx, is_ctx):
    return jnp.where(is_ctx, mc_ref[idx:idx + 1, :], ml_ref[idx:idx + 1, :])


def _inproj_kernel(x_ref, ml_ref, mc_ref, w_ref, *rest, tm, n_lat, rope_blocks):
    if rope_blocks:
        cos_ref, sin_ref, o_ref, h_ref = rest
    else:
        o_ref, h_ref = rest
    j = pl.program_id(1)
    n = pl.program_id(2)

    @pl.when(n == 0)
    def _():
        row = j * tm + lax.broadcasted_iota(jnp.int32, (tm, 1), 0)
        is_ctx = row >= n_lat
        shift = _select_mod(ml_ref, mc_ref, 0, is_ctx)
        scale = _select_mod(ml_ref, mc_ref, 1, is_ctx)
        h_ref[...] = (x_ref[...] * (1.0 + scale) + shift).astype(h_ref.dtype)

    acc = _dot(h_ref[...], w_ref[...])
    if not rope_blocks:
        o_ref[...] = acc.astype(o_ref.dtype)
        return

    @pl.when(n >= rope_blocks)
    def _():
        o_ref[...] = acc.astype(o_ref.dtype)

    @pl.when(n < rope_blocks)
    def _():
        q_scale = jnp.where(n == 0, (DH_B ** -0.5) * LOG2_E, 1.0)
        cos = cos_ref[...] * q_scale
        sin = sin_ref[...] * q_scale
        lane = lax.broadcasted_iota(jnp.int32, (1, LANES), 1)
        first_half = (lane & 31) < 16
        for i in range(acc.shape[1] // LANES):
            xb = acc[:, i * LANES:(i + 1) * LANES]
            partner = jnp.where(first_half, pltpu.roll(xb, LANES - 16, axis=1), pltpu.roll(xb, 16, axis=1))
            o_ref[:, i * LANES:(i + 1) * LANES] = (xb * cos + partner * sin).astype(o_ref.dtype)


def _inproj(xs, modtab, layer, w, n_lat, rope=None, rope_blocks=0, out_dtype=F32):
    b, nt, d = xs.shape
    f = w.shape[1]
    tm = _row_tile(nt)
    tn = min(f, 1024)
    lat, ctx = _mod_specs(layer, b)
    in_specs = [pl.BlockSpec((None, tm, d), lambda bi, j, n: (bi, j, 0)),
                pl.BlockSpec((None, None, 6, d), lat),
                pl.BlockSpec((None, None, 6, d), ctx),
                pl.BlockSpec((d, tn), lambda bi, j, n: (0, n))]
    args = [xs, modtab, modtab, w]
    if rope_blocks:
        in_specs += [pl.BlockSpec((tm, LANES), lambda bi, j, n: (j, 0))] * 2
        args += list(rope)
    return pl.pallas_call(
        functools.partial(_inproj_kernel, tm=tm, n_lat=n_lat, rope_blocks=rope_blocks),
        grid=(b, nt // tm, f // tn),
        in_specs=in_specs,
        out_specs=pl.BlockSpec((None, tm, tn), lambda bi, j, n: (bi, j, n)),
        out_shape=jax.ShapeDtypeStruct((b, nt, f), out_dtype),
        scratch_shapes=[pltpu.VMEM((tm, d), BF16)],
        compiler_params=_params(("parallel", "parallel", "arbitrary")),
        name="inproj",
    )(*args)


def _layer_norm(v, g, b):
    mu = jnp.mean(v, axis=-1, keepdims=True)
    vc = v - mu
    var = jnp.mean(vc * vc, axis=-1, keepdims=True)
    return vc * lax.rsqrt(var + LN_EPS) * g + b


def _post_kernel(x_ref, ml_ref, mc_ref, o_ref, *rest, tm, n_lat, group, gate, out_scale):
    if gate is None:
        gn_ref, w_ref, lg_ref, lb_ref, out_ref = rest
    else:
        z_ref, gn_ref, w_ref, lg_ref, lb_ref, out_ref = rest
    j = pl.program_id(1)
    row = j * tm + lax.broadcasted_iota(jnp.int32, (tm, 1), 0)
    is_ctx = row >= n_lat
    width = o_ref.shape[-1]
    parts = []
    for g0 in range(0, width, group):
        og = o_ref[:, g0:g0 + group]
        ms = jnp.mean(og * og, axis=-1, keepdims=True)
        yg = og * lax.rsqrt(ms + EPS) * gn_ref[:, g0:g0 + group]
        if gate == "silu":
            z = z_ref[:, g0:g0 + group]
            yg = yg * (z * _sigmoid(z))
        elif gate == "sigmoid":
            yg = yg * _sigmoid(z_ref[:, g0:g0 + group])
        else:
            yg = yg * out_scale
        parts.append(yg.astype(BF16))
    y = _dot(jnp.concatenate(parts, axis=-1), w_ref[...])
    res_gate = _select_mod(ml_ref, mc_ref, 2, is_ctx)
    out_ref[...] = _layer_norm(ALPHA * x_ref[...] + res_gate * y, lg_ref[...], lb_ref[...])


def _post(xs, modtab, layer, o, z_arr, z_block, gnorm, w_out, ln_g, ln_b, n_lat, rows, group, gate, out_scale=1.0):
    b, _, d = xs.shape
    width = o.shape[-1]
    tm = _row_tile(rows)
    lat, ctx = _mod_specs(layer, b)
    in_specs = [pl.BlockSpec((None, tm, d), lambda bi, j: (bi, j, 0)),
                pl.BlockSpec((None, None, 6, d), lat),
                pl.BlockSpec((None, None, 6, d), ctx),
                pl.BlockSpec((None, tm, width), lambda bi, j: (bi, j, 0))]
    args = [xs, modtab, modtab, o]
    if gate is not None:
        in_specs.append(pl.BlockSpec((None, tm, width), lambda bi, j: (bi, j, z_block)))
        args.append(z_arr)
    in_specs += [pl.BlockSpec((1, width), lambda bi, j: (0, 0)),
                 pl.BlockSpec((width, d), lambda bi, j: (0, 0)),
                 pl.BlockSpec((1, d), lambda bi, j: (0, 0)),
                 pl.BlockSpec((1, d), lambda bi, j: (0, 0))]
    args += [gnorm.reshape(1, width), w_out, ln_g.reshape(1, d), ln_b.reshape(1, d)]
    return pl.pallas_call(
        functools.partial(_post_kernel, tm=tm, n_lat=n_lat, group=group, gate=gate, out_scale=out_scale),
        grid=(b, rows // tm),
        in_specs=in_specs,
        out_specs=pl.BlockSpec((None, tm, d), lambda bi, j: (bi, j, 0)),
        out_shape=jax.ShapeDtypeStruct((b, rows, d), F32),
        compiler_params=_params(("parallel", "parallel")),
        name="mixer_out",
    )(*args)


def _mlp_kernel(x_ref, ml_ref, mc_ref, w1_ref, w2_ref, lg_ref, lb_ref, out_ref, h_ref, acc_ref, *, tm, n_lat):
    j = pl.program_id(1)
    k = pl.program_id(2)
    row = j * tm + lax.broadcasted_iota(jnp.int32, (tm, 1), 0)
    is_ctx = row >= n_lat

    @pl.when(k == 0)
    def _():
        shift = _select_mod(ml_ref, mc_ref, 3, is_ctx)
        scale = _select_mod(ml_ref, mc_ref, 4, is_ctx)
        h_ref[...] = (x_ref[...] * (1.0 + scale) + shift).astype(h_ref.dtype)
        acc_ref[...] = jnp.zeros_like(acc_ref)

    a = jnp.maximum(_dot(h_ref[...], w1_ref[...]), 0.0)
    acc_ref[...] += _dot((a * a).astype(BF16), w2_ref[...])

    @pl.when(k == pl.num_programs(2) - 1)
    def _():
        res_gate = _select_mod(ml_ref, mc_ref, 5, is_ctx)
        out_ref[...] = _layer_norm(ALPHA * x_ref[...] + res_gate * acc_ref[...], lg_ref[...], lb_ref[...])


def _mlp(xs, modtab, layer, w1, w2, ln_g, ln_b, n_lat):
    b, rows, d = xs.shape
    dff = w1.shape[1]
    tm = _row_tile(rows)
    tf = 2048
    lat, ctx = _mod_specs(layer, b)
    return pl.pallas_call(
        functools.partial(_mlp_kernel, tm=tm, n_lat=n_lat),
        grid=(b, rows // tm, dff // tf),
        in_specs=[pl.BlockSpec((None, tm, d), lambda bi, j, k: (bi, j, 0)),
                  pl.BlockSpec((None, None, 6, d), lat),
                  pl.BlockSpec((None, None, 6, d), ctx),
                  pl.BlockSpec((d, tf), lambda bi, j, k: (0, k)),
                  pl.BlockSpec((tf, d), lambda bi, j, k: (k, 0)),
                  pl.BlockSpec((1, d), lambda bi, j, k: (0, 0)),
                  pl.BlockSpec((1, d), lambda bi, j, k: (0, 0))],
        out_specs=pl.BlockSpec((None, tm, d), lambda bi, j, k: (bi, j, 0)),
        out_shape=jax.ShapeDtypeStruct((b, rows, d), F32),
        scratch_shapes=[pltpu.VMEM((tm, d), BF16), pltpu.VMEM((tm, d), F32)],
        compiler_params=_params(("parallel", "parallel", "arbitrary")),
        name="mlp",
    )(xs, modtab, modtab, w1, w2, ln_g.reshape(1, d), ln_b.reshape(1, d))


def _segment_orders(n_lat_seg, n_ctx_seg):
    def fwd(j):
        return jnp.where(j < n_ctx_seg, j + n_lat_seg, j - n_ctx_seg)

    def bwd(j):
        return n_lat_seg + n_ctx_seg - 1 - j
    return fwd, bwd


def _cumsum_mats():
    r = lax.broadcasted_iota(jnp.int32, (CHUNK, CHUNK), 0)
    c = lax.broadcasted_iota(jnp.int32, (CHUNK, CHUNK), 1)
    return (c <= r).astype(F32), (c >= r).astype(F32)


def _chunk_cumsums(val, fwd_cols, bwd_cols, rows):
    lower, upper = _cumsum_mats()
    out = []
    for r0 in range(0, rows, CHUNK):
        v = val[r0:r0 + CHUNK]
        cf = jnp.dot(lower, v, precision=HIGHEST, preferred_element_type=F32)
        cb = jnp.dot(upper, v, precision=HIGHEST, preferred_element_type=F32)
        out.append(jnp.where(fwd_cols, cf, jnp.where(bwd_cols, cb, v)))
    return jnp.concatenate(out, axis=0)


def _pick_col(tile, idx):
    lane = lax.broadcasted_iota(jnp.int32, tile.shape, 1)
    return jnp.sum(jnp.where(lane == idx, tile, 0.0), axis=1, keepdims=True)


def _pick_row(tile, idx):
    sub = lax.broadcasted_iota(jnp.int32, tile.shape, 0)
    return jnp.sum(jnp.where(sub == idx, tile, 0.0), axis=0, keepdims=True)


def _tri_masks(direction):
    r = lax.broadcasted_iota(jnp.int32, (CHUNK, CHUNK), 0)
    c = lax.broadcasted_iota(jnp.int32, (CHUNK, CHUNK), 1)
    if direction == 0:
        return c <= r, c < r
    return c >= r, c > r


def _unit_tri_inverses(mats):
    r = lax.broadcasted_iota(jnp.int32, (CHUNK, CHUNK), 0)
    c = lax.broadcasted_iota(jnp.int32, (CHUNK, CHUNK), 1)
    differ = r ^ c
    eye = jnp.where(r == c, 1.0, 0.0)
    ts = [eye - jnp.where(differ < 2, a, 0.0) for a in mats]
    size = 2
    while size < CHUNK:
        joins = (differ >= size) & (differ < 2 * size)
        ets = [_dot(jnp.where(joins, a, 0.0), t) for a, t in zip(mats, ts)]
        ts = [t - _dot(t, et) for t, et in zip(ts, ets)]
        size *= 2
    return ts


def _gdn_gate_kernel(gp_ref, alog_ref, dtb_ref, o_ref, *, rows):
    x = gp_ref[...]
    lane = lax.broadcasted_iota(jnp.int32, (1, LANES), 1)
    live = lane < 4 * H_A
    is_decay = live & ((lane & (2 * H_A - 1)) < H_A)
    g = -jnp.exp(alog_ref[...]) * _softplus(x + dtb_ref[...])
    val = jnp.where(is_decay, g, _sigmoid(x))
    o_ref[...] = _chunk_cumsums(val, is_decay & (lane < 2 * H_A), is_decay & (lane >= 2 * H_A), rows)


def _gdn_gates(gp, a_log, dt_bias):
    b, nt, _ = gp.shape
    rows = 256
    pad = lambda t: jnp.pad(jnp.stack([t[0], jnp.zeros_like(t[0]), t[1], jnp.zeros_like(t[1])]).reshape(1, -1),
                            ((0, 0), (0, LANES - 4 * H_A)))
    return pl.pallas_call(
        functools.partial(_gdn_gate_kernel, rows=rows),
        grid=(b, nt // rows),
        in_specs=[pl.BlockSpec((None, rows, LANES), lambda bi, j: (bi, j, 0)),
                  pl.BlockSpec((1, LANES), lambda bi, j: (0, 0)),
                  pl.BlockSpec((1, LANES), lambda bi, j: (0, 0))],
        out_specs=pl.BlockSpec((None, rows, LANES), lambda bi, j: (bi, j, 0)),
        out_shape=jax.ShapeDtypeStruct((b, nt, LANES), F32),
        compiler_params=_params(("parallel", "parallel")),
        name="gdn_gates",
    )(gp, pad(a_log.astype(F32)), pad(dt_bias.astype(F32)))


def _gdn_conv_kernel(p_ref, w_ref, o_ref, pad_ref, *, n_lat, nt):
    half = CONV_K // 2
    width = p_ref.shape[-1]
    zeros = jnp.zeros((SUBLANES, width), F32)
    segments = ((0, n_lat, SUBLANES), (n_lat, nt, 2 * SUBLANES))
    pad_ref[0:SUBLANES, :] = zeros
    for lo, hi, off in segments:
        pad_ref[lo + off:hi + off, :] = p_ref[lo:hi, :]
        pad_ref[hi + off:hi + off + SUBLANES, :] = zeros
    for g in range(width // DK_A):
        head = pl.program_id(1) * (width // DK_A) + g
        lanes = slice(g * DK_A, (g + 1) * DK_A)
        for lo, hi, off in segments:
            acc = p_ref[lo:hi, lanes] * w_ref[half:half + 1, lanes]
            for s in range(-half, half + 1):
                if s != 0:
                    acc = acc + pad_ref[lo + off + s:hi + off + s, lanes] * w_ref[half + s:half + s + 1, lanes]
            y = acc * _sigmoid(acc)
            inv = lax.rsqrt(jnp.sum(y * y, axis=-1, keepdims=True) + EPS)
            factor = jnp.where(head < H_A, inv * (DK_A ** -0.5), jnp.where(head < 2 * H_A, inv, 1.0))
            o_ref[lo:hi, lanes] = y * factor


def _gdn_conv(p, conv_w, n_lat):
    b, nt, _ = p.shape
    width = DK_A
    nblk = 3 * H_A * DK_A // width
    return pl.pallas_call(
        functools.partial(_gdn_conv_kernel, n_lat=n_lat, nt=nt),
        grid=(b, nblk),
        in_specs=[pl.BlockSpec((None, nt, width), lambda bi, cb: (bi, 0, cb)),
                  pl.BlockSpec((CONV_K, width), lambda bi, cb: (0, cb))],
        out_specs=pl.BlockSpec((None, nt, width), lambda bi, cb: (bi, 0, cb)),
        out_shape=jax.ShapeDtypeStruct((b, nt, nblk * width), F32),
        scratch_shapes=[pltpu.VMEM((nt + 3 * SUBLANES, width), F32)],
        compiler_params=_params(("parallel", "parallel")),
        name="gdn_conv",
    )(p, conv_w.astype(F32))


def _gdn_intra_kernel(q_ref, k_ref, v_ref, g_ref, u_ref, w_ref, qg_ref, kd_ref, qk_ref, gl_ref, *, rows):
    h = pl.program_id(1)
    chains = []
    for pair in range(rows // (2 * CHUNK)):
        p0 = pair * 2 * CHUNK
        gates = g_ref[p0:p0 + 2 * CHUNK, :]
        gates_t = gates.T
        for sub in range(2):
            r0 = p0 + sub * CHUNK
            q = q_ref[r0:r0 + CHUNK, :]
            k = k_ref[r0:r0 + CHUNK, :]
            v = v_ref[r0:r0 + CHUNK, :]
            kk = _dot_nt(k, k)
            qk = _dot_nt(q, k)
            tile = gates[sub * CHUNK:(sub + 1) * CHUNK, :]
            for d in range(2):
                tri, strict = _tri_masks(d)
                gcol = _pick_col(tile, 2 * H_A * d + h)
                beta = _pick_col(tile, 2 * H_A * d + H_A + h)
                grow = _pick_row(gates_t, 2 * H_A * d + h)[:, sub * CHUNK:(sub + 1) * CHUNK]
                decay = jnp.where(tri, jnp.exp(jnp.where(tri, gcol - grow, 0.0)), 0.0)
                eg = jnp.exp(gcol)
                last = CHUNK - 1 if d == 0 else 0
                g_last = gcol[last:last + 1, :]
                qg_ref[d, r0:r0 + CHUNK, :] = q * eg
                kd_ref[d, r0:r0 + CHUNK, :] = k * jnp.exp(g_last - gcol)
                qk_ref[r0:r0 + CHUNK, d * CHUNK:(d + 1) * CHUNK] = jnp.where(tri, qk * decay, 0.0)
                gl_ref[d, r0 // CHUNK] = jnp.broadcast_to(jnp.exp(g_last), (1, LANES))
                chains.append((d, r0, jnp.where(strict, kk * beta * decay, 0.0),
                               jnp.concatenate([v * beta, k * (beta * eg)], axis=1)))
    t_invs = _unit_tri_inverses([a for _, _, a, _ in chains])
    for (d, r0, _, rhs), t_inv in zip(chains, t_invs):
        uw = _dot(t_inv, rhs)
        u_ref[d, r0:r0 + CHUNK, :] = uw[:, :DK_A]
        w_ref[d, r0:r0 + CHUNK, :] = uw[:, DK_A:]


def _gdn_intra(qkv, gates):
    b, nt, _ = qkv.shape
    rows = _row_tile(nt)
    nch = nt // CHUNK
    big = jax.ShapeDtypeStruct((b, H_A, 2, nt, DK_A), F32)
    big_spec = pl.BlockSpec((None, None, 2, rows, DK_A), lambda bi, h, j: (bi, h, 0, j, 0))
    col = lambda off: pl.BlockSpec((None, rows, DK_A), lambda bi, h, j: (bi, j, off + h))
    return pl.pallas_call(
        functools.partial(_gdn_intra_kernel, rows=rows),
        grid=(b, H_A, nt // rows),
        in_specs=[col(0), col(H_A), col(2 * H_A),
                  pl.BlockSpec((None, rows, LANES), lambda bi, h, j: (bi, j, 0))],
        out_specs=[big_spec, big_spec, big_spec, big_spec,
                   pl.BlockSpec((None, None, rows, 2 * CHUNK), lambda bi, h, j: (bi, h, j, 0)),
                   pl.BlockSpec((None, None, 2, rows // CHUNK, 1, LANES), lambda bi, h, j: (bi, h, 0, j, 0, 0))],
        out_shape=[big, big, big, big,
                   jax.ShapeDtypeStruct((b, H_A, nt, 2 * CHUNK), F32),
                   jax.ShapeDtypeStruct((b, H_A, 2, nch, 1, LANES), F32)],
        compiler_params=_params(("parallel", "parallel", "parallel")),
        name="gdn_intra",
    )(qkv, qkv, qkv, gates)


def _gdn_scan_kernel(*refs, n_lat_seg, n_ctx_seg):
    dir_refs = (refs[0:6], refs[6:12])
    o_ref, s_ref = refs[12:]
    j = pl.program_id(1)
    fwd_seg, bwd_seg = _segment_orders(n_lat_seg, n_ctx_seg)
    seg_row0 = (pl.multiple_of(fwd_seg(j) * SCAN_SEG, SCAN_SEG), pl.multiple_of(bwd_seg(j) * SCAN_SEG, SCAN_SEG))

    @pl.when(j == 0)
    def _():
        s_ref[...] = jnp.zeros_like(s_ref)
        o_ref[...] = jnp.zeros_like(o_ref)

    nsub = SCAN_SEG // CHUNK
    for sub in range(nsub):
        chains = []
        for d in range(2):
            c = sub if d == 0 else nsub - 1 - sub
            rows = slice(c * CHUNK, (c + 1) * CHUNK)
            _, w_ref, qg_ref, _, _, _ = dir_refs[d]
            for h in range(H_A):
                state = s_ref[d, h]
                wq = _dot(jnp.concatenate([w_ref[h, rows, :], qg_ref[h, rows, :]], axis=0), state)
                chains.append((d, h, c, rows, state, wq))
        for d, h, c, rows, state, wq in chains:
            u_ref, _, _, kd_ref, qk_ref, gl_ref = dir_refs[d]
            v_new = u_ref[h, rows, :] - wq[:CHUNK]
            qk = qk_ref[h, rows, d * CHUNK:(d + 1) * CHUNK]
            out_rows = pl.ds(seg_row0[d] + c * CHUNK, CHUNK)
            o_ref[out_rows, h * DK_A:(h + 1) * DK_A] += wq[CHUNK:] + _dot(qk, v_new)
            s_ref[d, h] = state * gl_ref[h, c] + _dot_tn(kd_ref[h, rows, :], v_new)


def _gdn_scan(u, w, qg, kd, qk, gl, n_lat):
    b, h, _, nt, dk = u.shape
    nsub = SCAN_SEG // CHUNK
    n_lat_seg, n_ctx_seg = n_lat // SCAN_SEG, (nt - n_lat) // SCAN_SEG
    orders = _segment_orders(n_lat_seg, n_ctx_seg)
    in_specs = []
    for d in range(2):
        seg = orders[d]
        big = pl.BlockSpec((None, h, None, SCAN_SEG, dk), lambda bi, j, d=d, seg=seg: (bi, 0, d, seg(j), 0))
        in_specs += [big, big, big, big,
                     pl.BlockSpec((None, h, SCAN_SEG, 2 * CHUNK), lambda bi, j, seg=seg: (bi, 0, seg(j), 0)),
                     pl.BlockSpec((None, h, None, nsub, 1, LANES), lambda bi, j, d=d, seg=seg: (bi, 0, d, seg(j), 0, 0))]
    return pl.pallas_call(
        functools.partial(_gdn_scan_kernel, n_lat_seg=n_lat_seg, n_ctx_seg=n_ctx_seg),
        grid=(b, nt // SCAN_SEG),
        in_specs=in_specs,
        out_specs=pl.BlockSpec((None, nt, h * dk), lambda bi, j: (bi, 0, 0)),
        out_shape=jax.ShapeDtypeStruct((b, nt, h * dk), F32),
        scratch_shapes=[pltpu.VMEM((2, h, dk, dk), F32)],
        compiler_params=_params(("parallel", "arbitrary")),
        name="gdn_scan",
    )(u, w, qg, kd, qk, gl, u, w, qg, kd, qk, gl)


def _gated_deltanet(xs, modtab, layer, n_lat, w_in, conv_w, a_log, dt_bias):
    key = H_A * DK_A
    w_main = w_in[:, :4 * key].astype(BF16)
    w_gate = jnp.pad(w_in[:, 4 * key:], ((0, 0), (0, LANES - 4 * H_A))).astype(BF16)
    p = _inproj(xs, modtab, layer, w_main, n_lat)
    gp = _inproj(xs, modtab, layer, w_gate, n_lat)
    gates = _gdn_gates(gp, a_log, dt_bias)
    qkv = _gdn_conv(p, conv_w, n_lat)
    u, w, qg, kd, qk, gl = _gdn_intra(qkv, gates)
    o = _gdn_scan(u, w, qg, kd, qk, gl, n_lat)
    return o, p


def _rope_tables(n_lat, nt):
    rows = n_lat // GRID_W
    r = jnp.repeat(jnp.arange(rows, dtype=F32), GRID_W)
    col = jnp.tile(jnp.arange(GRID_W, dtype=F32), rows)
    half = DH_B // 2
    inv = ROPE_BASE ** (-jnp.arange(0, half, 2, dtype=F32) / half)
    ang_r, ang_c = r[:, None] * inv, col[:, None] * inv
    cos = jnp.concatenate([jnp.cos(ang_r)] * 2 + [jnp.cos(ang_c)] * 2, axis=-1)
    sin = jnp.concatenate([-jnp.sin(ang_r), jnp.sin(ang_r), -jnp.sin(ang_c), jnp.sin(ang_c)], axis=-1)
    cos = jnp.concatenate([jnp.tile(cos, (1, LANES // DH_B)), jnp.ones((nt - n_lat, LANES), F32)], axis=0)
    sin = jnp.concatenate([jnp.tile(sin, (1, LANES // DH_B)), jnp.zeros((nt - n_lat, LANES), F32)], axis=0)
    return cos, sin


def _attn_kernel(q_ref, k_ref, v_ref, lam_ref, o_ref, *, tq, n_lat, nt, lam_init):
    j = pl.program_id(2)
    lam = lam_ref[...]
    lam_full = (jnp.exp(jnp.sum(lam[0:1] * lam[1:2], axis=-1, keepdims=True))
                - jnp.exp(jnp.sum(lam[2:3] * lam[3:4], axis=-1, keepdims=True)) + lam_init)
    dv = 2 * DH_B
    lane = lax.broadcasted_iota(jnp.int32, (1, dv), 1)

    def attend(key_lo, key_hi):
        block = min(ATTN_KEY_BLOCK, key_hi - key_lo)
        ones = jnp.ones((block, dv), BF16)
        for r0 in range(0, tq, ATTN_QUERY_ROWS):
            q = q_ref[r0:r0 + ATTN_QUERY_ROWS, :]
            q_maps = jnp.concatenate([jnp.where(lane < DH_B, q, 0), jnp.where(lane >= DH_B, q, 0)], axis=0)
            m = acc = None
            for k0 in range(key_lo, key_hi, block):
                s = _dot_nt(q_maps, k_ref[k0:k0 + block, :])
                v_ext = jnp.concatenate([v_ref[k0:k0 + block, :], ones], axis=1)
                s_max = jnp.max(s, axis=-1, keepdims=True)
                if m is None:
                    m = s_max
                    acc = _dot(jnp.exp2(s - m).astype(BF16), v_ext)
                else:
                    m_new = jnp.maximum(m, s_max)
                    acc = jnp.exp2(m - m_new) * acc + _dot(jnp.exp2(s - m_new).astype(BF16), v_ext)
                    m = m_new
            normalised = acc[:, :dv] / acc[:, dv:]
            o_ref[r0:r0 + ATTN_QUERY_ROWS, :] = (normalised[:ATTN_QUERY_ROWS]
                                                 - lam_full * normalised[ATTN_QUERY_ROWS:])

    @pl.when(j * tq < n_lat)
    def _():
        attend(0, nt)

    @pl.when(j * tq >= n_lat)
    def _():
        attend(n_lat, nt)


def _diff_attention(p, lam, n_lat, lam_init):
    b, nt, _ = p.shape
    tq = 256
    dv = 2 * DH_B
    return pl.pallas_call(
        functools.partial(_attn_kernel, tq=tq, n_lat=n_lat, nt=nt, lam_init=lam_init),
        grid=(b, H_B, nt // tq),
        in_specs=[pl.BlockSpec((None, tq, dv), lambda bi, h, j: (bi, j, h)),
                  pl.BlockSpec((None, nt, dv), lambda bi, h, j: (bi, 0, H_B + h)),
                  pl.BlockSpec((None, nt, dv), lambda bi, h, j: (bi, 0, 2 * H_B + h)),
                  pl.BlockSpec((4, DH_B), lambda bi, h, j: (0, 0))],
        out_specs=pl.BlockSpec((None, tq, dv), lambda bi, h, j: (bi, j, h)),
        out_shape=jax.ShapeDtypeStruct((b, nt, H_B * dv), F32),
        compiler_params=_params(("parallel", "parallel", "arbitrary")),
        name="diff_attn",
    )(p, p, p, lam.astype(F32))


def _mlstm_gate_kernel(gp_ref, bias_ref, o_ref, m0_ref, *, nt, n_lat):
    x = gp_ref[...] + bias_ref[...]
    lane = lax.broadcasted_iota(jnp.int32, (1, LANES), 1)
    live = lane < 4 * H_C
    is_forget = live & ((lane & H_C) != 0)
    fwd_forget = is_forget & (lane < 2 * H_C)
    val = jnp.where(is_forget, -_softplus(-x), x)
    gates = _chunk_cumsums(val, fwd_forget, is_forget & (lane >= 2 * H_C), nt)
    o_ref[...] = gates
    nch, n_lat_ch = nt // CHUNK, n_lat // CHUNK
    b_last, m_end = [], []
    for c in range(nch):
        tile = gates[c * CHUNK:(c + 1) * CHUNK, :]
        total = jnp.where(fwd_forget, tile[CHUNK - 1:CHUNK, :], tile[0:1, :])
        a_end = total - tile + pltpu.roll(tile, H_C, axis=1)
        b_last.append(total)
        m_end.append(jnp.max(a_end, axis=0, keepdims=True))
    orders = (list(range(n_lat_ch, nch)) + list(range(n_lat_ch)), list(range(nch - 1, -1, -1)))
    for d in range(2):
        m = jnp.zeros((1, LANES), F32)
        for c in orders[d]:
            m0_ref[d, c] = m
            m = jnp.maximum(b_last[c] + m, m_end[c])


def _mlstm_gates(gp, gate_bias, n_lat):
    b, nt, _ = gp.shape
    nch = nt // CHUNK
    bias = jnp.pad(gate_bias.astype(F32).reshape(1, -1), ((0, 0), (0, LANES - 4 * H_C)))
    return pl.pallas_call(
        functools.partial(_mlstm_gate_kernel, nt=nt, n_lat=n_lat),
        grid=(b,),
        in_specs=[pl.BlockSpec((None, nt, LANES), lambda bi: (bi, 0, 0)),
                  pl.BlockSpec((1, LANES), lambda bi: (0, 0))],
        out_specs=[pl.BlockSpec((None, nt, LANES), lambda bi: (bi, 0, 0)),
                   pl.BlockSpec((None, 2, nch, 1, LANES), lambda bi: (bi, 0, 0, 0, 0))],
        out_shape=[jax.ShapeDtypeStruct((b, nt, LANES), F32),
                   jax.ShapeDtypeStruct((b, 2, nch, 1, LANES), F32)],
        compiler_params=_params(("parallel",)),
        name="mlstm_gates",
    )(gp, bias)


def _mlstm_intra_kernel(q_ref, k_ref, v_ref, g_ref, m0_ref, qs_ref, num_ref, ke_ref, st_ref, so_ref, *, rows):
    h = pl.program_id(1)
    for pair in range(rows // (2 * CHUNK)):
        p0 = pair * 2 * CHUNK
        gates = g_ref[p0:p0 + 2 * CHUNK, :]
        gates_t = gates.T
        for sub in range(2):
            r0 = p0 + sub * CHUNK
            q = q_ref[r0:r0 + CHUNK, :]
            k = k_ref[r0:r0 + CHUNK, :] * (DQK_C ** -0.5)
            v = v_ref[r0:r0 + CHUNK, :]
            qk = _dot_nt(q, k)
            tile = gates[sub * CHUNK:(sub + 1) * CHUNK, :]
            for d in range(2):
                tri, _ = _tri_masks(d)
                li_idx = 2 * d * H_C + h
                bc_idx = (2 * d + 1) * H_C + h
                bcol = _pick_col(tile, bc_idx)
                licol = _pick_col(tile, li_idx)
                brow = _pick_row(gates_t, bc_idx)[:, sub * CHUNK:(sub + 1) * CHUNK]
                lirow = _pick_row(gates_t, li_idx)[:, sub * CHUNK:(sub + 1) * CHUNK]
                m0 = _pick_col(m0_ref[d, r0 // CHUNK], bc_idx)
                dmat = jnp.where(tri, bcol - brow + lirow, -jnp.inf)
                m_intra = jnp.max(dmat, axis=-1, keepdims=True)
                m_t = jnp.maximum(bcol + m0, m_intra)
                wts = jnp.exp(dmat - m_t) * qk
                last = CHUNK - 1 if d == 0 else 0
                b_last = bcol[last:last + 1, :]
                m_new = jnp.maximum(b_last + m0, m_intra[last:last + 1, :])
                qs_ref[d, r0:r0 + CHUNK, :] = q * jnp.exp(bcol + m0 - m_t)
                num_ref[d, r0:r0 + CHUNK, :] = _dot(wts, v)
                ke_ref[d, r0:r0 + CHUNK, :] = k * jnp.exp(b_last - bcol + licol - m_new)
                den = jnp.sum(wts, axis=-1, keepdims=True)
                st_ref[d, r0:r0 + CHUNK, 0:LANES] = jnp.broadcast_to(den, (CHUNK, LANES))
                st_ref[d, r0:r0 + CHUNK, LANES:2 * LANES] = jnp.broadcast_to(jnp.exp(-m_t), (CHUNK, LANES))
                so_ref[d, r0 // CHUNK] = jnp.broadcast_to(jnp.exp(b_last + m0 - m_new), (1, LANES))


def _mlstm_intra(p, gates, m0):
    b, nt, _ = p.shape
    rows = 256
    qk_w = H_C * DQK_C
    nch = nt // CHUNK
    q_spec = pl.BlockSpec((None, rows, DQK_C), lambda bi, h, j: (bi, j, h))
    k_spec = pl.BlockSpec((None, rows, DQK_C), lambda bi, h, j: (bi, j, H_C + h))
    v_spec = pl.BlockSpec((None, rows, DV_C), lambda bi, h, j: (bi, j, 2 * qk_w // DV_C + h))
    per_chunk = lambda: pl.BlockSpec((None, None, 2, rows // CHUNK, 1, LANES), lambda bi, h, j: (bi, h, 0, j, 0, 0))
    out = lambda w: (jax.ShapeDtypeStruct((b, H_C, 2, nt, w), F32),
                     pl.BlockSpec((None, None, 2, rows, w), lambda bi, h, j: (bi, h, 0, j, 0)))
    (qs_s, qs_b), (num_s, num_b), (ke_s, ke_b), (st_s, st_b) = out(DQK_C), out(DV_C), out(DQK_C), out(2 * LANES)
    return pl.pallas_call(
        functools.partial(_mlstm_intra_kernel, rows=rows),
        grid=(b, H_C, nt // rows),
        in_specs=[q_spec, k_spec, v_spec, pl.BlockSpec((None, rows, LANES), lambda bi, h, j: (bi, j, 0)),
                  pl.BlockSpec((None, 2, rows // CHUNK, 1, LANES), lambda bi, h, j: (bi, 0, j, 0, 0))],
        out_specs=[qs_b, num_b, ke_b, st_b, per_chunk()],
        out_shape=[qs_s, num_s, ke_s, st_s, jax.ShapeDtypeStruct((b, H_C, 2, nch, 1, LANES), F32)],
        compiler_params=_params(("parallel", "parallel", "parallel")),
        name="mlstm_intra",
    )(p, p, p, gates, m0)


def _mlstm_scan_kernel(*refs, n_lat_seg, n_ctx_seg):
    dir_refs = (refs[0:6], refs[6:12])
    o_ref, c_ref = refs[12:]
    j = pl.program_id(1)
    fwd_seg, bwd_seg = _segment_orders(n_lat_seg, n_ctx_seg)
    seg_row0 = (pl.multiple_of(fwd_seg(j) * SCAN_SEG, SCAN_SEG), pl.multiple_of(bwd_seg(j) * SCAN_SEG, SCAN_SEG))

    @pl.when(j == 0)
    def _():
        c_ref[...] = jnp.zeros_like(c_ref)
        o_ref[...] = jnp.zeros_like(o_ref)

    ones = jnp.ones((CHUNK, LANES), F32)
    nsub = SCAN_SEG // CHUNK
    for sub in range(nsub):
        chains = []
        for d in range(2):
            c = sub if d == 0 else nsub - 1 - sub
            rows = slice(c * CHUNK, (c + 1) * CHUNK)
            qs_ref = dir_refs[d][1]
            for h in range(H_C):
                c_st = c_ref[d, h]
                chains.append((d, h, c, rows, c_st, _dot(qs_ref[h, rows, :], c_st)))
        for d, h, c, rows, c_st, q_c in chains:
            v_ref, _, num_ref, ke_ref, st_ref, so_ref = dir_refs[d]
            num = q_c[:, :DV_C] + num_ref[h, rows, :]
            den = q_c[:, DV_C:] + st_ref[h, rows, 0:LANES]
            inv = 1.0 / jnp.maximum(jnp.abs(den), st_ref[h, rows, LANES:2 * LANES])
            out_rows = pl.ds(seg_row0[d] + c * CHUNK, CHUNK)
            o_ref[out_rows, h * DV_C:(h + 1) * DV_C] += num * jnp.concatenate([inv] * (DV_C // LANES), axis=1)
            decay = jnp.concatenate([so_ref[h, c]] * (DV_C // LANES + 1), axis=1)
            v_ones = jnp.concatenate([v_ref[rows, h * DV_C:(h + 1) * DV_C], ones], axis=1)
            c_ref[d, h] = c_st * decay + _dot_tn(ke_ref[h, rows, :], v_ones)


def _mlstm_scan(p, qs, num, ke, st, so, n_lat):
    b, nt, _ = p.shape
    qk_w, val_w = H_C * DQK_C, H_C * DV_C
    nsub = SCAN_SEG // CHUNK
    n_lat_seg, n_ctx_seg = n_lat // SCAN_SEG, (nt - n_lat) // SCAN_SEG
    orders = _segment_orders(n_lat_seg, n_ctx_seg)
    in_specs = []
    for d in range(2):
        seg = orders[d]
        per_head = lambda w, d=d, seg=seg: pl.BlockSpec((None, H_C, None, SCAN_SEG, w),
                                                        lambda bi, j: (bi, 0, d, seg(j), 0))
        in_specs += [pl.BlockSpec((None, SCAN_SEG, val_w), lambda bi, j, seg=seg: (bi, seg(j), 2 * qk_w // val_w)),
                     per_head(DQK_C), per_head(DV_C), per_head(DQK_C), per_head(2 * LANES),
                     pl.BlockSpec((None, H_C, None, nsub, 1, LANES), lambda bi, j, d=d, seg=seg: (bi, 0, d, seg(j), 0, 0))]
    return pl.pallas_call(
        functools.partial(_mlstm_scan_kernel, n_lat_seg=n_lat_seg, n_ctx_seg=n_ctx_seg),
        grid=(b, nt // SCAN_SEG),
        in_specs=in_specs,
        out_specs=pl.BlockSpec((None, nt, val_w), lambda bi, j: (bi, 0, 0)),
        out_shape=jax.ShapeDtypeStruct((b, nt, val_w), F32),
        scratch_shapes=[pltpu.VMEM((2, H_C, DQK_C, DV_C + LANES), F32)],
        compiler_params=_params(("parallel", "arbitrary")),
        name="mlstm_scan",
    )(p, qs, num, ke, st, so, p, qs, num, ke, st, so)


def _mlstm(xs, modtab, layer, n_lat, w_in, gate_bias):
    main = 2 * H_C * DQK_C + 2 * H_C * DV_C
    w_main = w_in[:, :main].astype(BF16)
    w_gate = jnp.pad(w_in[:, main:], ((0, 0), (0, LANES - 4 * H_C))).astype(BF16)
    p = _inproj(xs, modtab, layer, w_main, n_lat)
    gp = _inproj(xs, modtab, layer, w_gate, n_lat)
    gates, m0 = _mlstm_gates(gp, gate_bias, n_lat)
    qs, num, ke, st, so = _mlstm_intra(p, gates, m0)
    return _mlstm_scan(p, qs, num, ke, st, so, n_lat), p


def kernel(x, c, ctx, c_ctx, w_mod, b_mod, ln_g, ln_b, w_in_a, conv_a, a_log_a, dt_bias_a, norm_a, w_out_a,
           w_in_b, lam_b, subln_b, w_out_b, w_in_c, gate_bias_c, norm_c, w_out_c, w1, w2):
    b, n_lat, d = x.shape
    nt = n_lat + ctx.shape[1]
    depth = w_mod.shape[0]
    xs = jnp.concatenate([x, ctx], axis=1)
    cond_rows = -(-(b + 1) // 8) * 8
    cc = jnp.concatenate([c, c_ctx[None, :], jnp.zeros((cond_rows - b - 1, d), F32)], axis=0)
    modtab = _mod_tables(cc, w_mod, b_mod)
    rope = _rope_tables(n_lat, nt)

    for i in range(depth):
        kind, j = i % N_MIXERS, i // N_MIXERS
        rows = nt if i < depth - 1 else n_lat
        if kind == 0:
            o, p = _gated_deltanet(xs, modtab, i, n_lat, w_in_a[j], conv_a[j], a_log_a[j], dt_bias_a[j])
            xs_new = _post(xs, modtab, i, o, p, 3, jnp.tile(norm_a[j], H_A), w_out_a[j].astype(BF16),
                           ln_g[i, 0], ln_b[i, 0], n_lat, rows, DK_A, "silu")
        elif kind == 1:
            lam_init = 0.8 - 0.6 * math.exp(-0.3 * i)
            p = _inproj(xs, modtab, i, w_in_b[j].astype(BF16), n_lat, rope=rope, rope_blocks=2, out_dtype=BF16)
            o = _diff_attention(p, lam_b[j], n_lat, lam_init)
            xs_new = _post(xs, modtab, i, o, None, 0, jnp.tile(subln_b[j], H_B), w_out_b[j].astype(BF16),
                           ln_g[i, 0], ln_b[i, 0], n_lat, rows, 2 * DH_B, None, out_scale=1.0 - lam_init)
        else:
            o, p = _mlstm(xs, modtab, i, n_lat, w_in_c[j], gate_bias_c[j])
            xs_new = _post(xs, modtab, i, o, p, 2, norm_c[j], w_out_c[j].astype(BF16),
                           ln_g[i, 0], ln_b[i, 0], n_lat, rows, DV_C, "sigmoid")
        xs = _mlp(xs_new, modtab, i, w1[i].astype(BF16), w2[i].astype(BF16), ln_g[i, 1], ln_b[i, 1], n_lat)
    return xs
```
